```python
import jax, jax.numpy as jnp
from jax import lax
import numpy as np

D_MODEL = 2048
BATCH = 2
SEQ = 8192
DEPTH = 2

N_MIXERS = 2
N_SC_LAYERS = (DEPTH + 1) // 2
N_SSD_LAYERS = DEPTH // 2
SC_CONV = 3
SSM_EXPAND = 2
D_INNER = SSM_EXPAND * D_MODEL
HEAD_DIM = 64
N_SSM_HEADS = D_INNER // HEAD_DIM
D_STATE = 128
N_GROUPS = 8
SSM_CONV = 3
CHUNK = 128
CONV_DIM = D_INNER + 2 * N_GROUPS * D_STATE
IN_PROJ_DIM = D_INNER + CONV_DIM + 2 * N_SSM_HEADS
D_FF = 4 * D_MODEL
EPS = 1e-6

kernel_name = "hybrid_shortconv_ssd_adaln_encoder"


def rms_norm(x, g):
    xf = x.astype(jnp.float32)
    y = xf * lax.rsqrt(jnp.mean(xf * xf, axis=-1, keepdims=True) + EPS)
    return (y * g.astype(jnp.float32)).astype(x.dtype)


def dwconv_centred(u, w):
    k = w.shape[1]
    pad = k // 2
    kern = jnp.transpose(w)[:, None, :].astype(u.dtype)
    return lax.conv_general_dilated(u, kern, window_strides=(1,), padding=[(pad, pad)],
                                    dimension_numbers=('NWC', 'WIO', 'NWC'),
                                    feature_group_count=u.shape[-1])


def short_conv_mixer(h, w_in, w_conv, w_out):
    b_gate, c_gate, v = jnp.split(h @ w_in, 3, axis=-1)
    u = dwconv_centred(c_gate * v, w_conv)
    return (b_gate * u) @ w_out


def ssd_chunked(x, dt, a_neg, bm, cm):
    b, L, H, P = x.shape
    G, N = bm.shape[2], bm.shape[3]
    R = H // G
    nc = L // CHUNK
    f32 = jnp.float32
    xs = jnp.moveaxis(x.astype(f32).reshape(b, nc, CHUNK, G, R, P), 1, 0)
    dts = jnp.moveaxis(dt.reshape(b, nc, CHUNK, G, R), 1, 0)
    av = dts * a_neg.reshape(G, R)
    bs = jnp.moveaxis(bm.astype(f32).reshape(b, nc, CHUNK, G, N), 1, 0)
    cs = jnp.moveaxis(cm.astype(f32).reshape(b, nc, CHUNK, G, N), 1, 0)
    mask = jnp.tril(jnp.ones((CHUNK, CHUNK), dtype=bool))[None, :, :, None, None]

    def step(state, inp):
        xc, ac, dtc, bc, cc = inp
        cum = jnp.cumsum(ac, axis=1)
        seg = cum[:, :, None] - cum[:, None, :]
        decay = jnp.exp(jnp.where(mask, seg, -jnp.inf))
        scores = jnp.einsum('btgn,bsgn->btsg', cc, bc)
        y_intra = jnp.einsum('btsg,btsgr,bsgr,bsgrp->btgrp', scores, decay, dtc, xc)
        y_state = jnp.einsum('btgn,bgrpn,btgr->btgrp', cc, state, jnp.exp(cum))
        last = cum[:, -1]
        w_end = jnp.exp(last[:, None] - cum) * dtc
        state = state * jnp.exp(last)[..., None, None] + jnp.einsum('bsgn,bsgr,bsgrp->bgrpn', bc, w_end, xc)
        return state, y_intra + y_state

    h0 = jnp.zeros((b, G, R, P, N), f32)
    _, y = lax.scan(step, h0, (xs, av, dts, bs, cs))
    return jnp.moveaxis(y, 0, 1).reshape(b, L, H, P)


def mamba2_bidir_mixer(h, w_in, conv_w, conv_b, dt_bias, a_log, d_skip, norm_g, w_out):
    bsz, L, _ = h.shape
    proj = h @ w_in
    z = proj[..., :D_INNER]
    xbc = proj[..., D_INNER:D_INNER + CONV_DIM]
    dt_raw = proj[..., D_INNER + CONV_DIM:]
    xbc = jax.nn.silu(dwconv_centred(xbc, conv_w) + conv_b.astype(xbc.dtype))
    xs = xbc[..., :D_INNER].reshape(bsz, L, N_SSM_HEADS, HEAD_DIM)
    bm = xbc[..., D_INNER:D_INNER + N_GROUPS * D_STATE].reshape(bsz, L, N_GROUPS, D_STATE)
    cm = xbc[..., D_INNER + N_GROUPS * D_STATE:].reshape(bsz, L, N_GROUPS, D_STATE)
    dt = jax.nn.softplus(dt_raw.astype(jnp.float32).reshape(bsz, L, 2, N_SSM_HEADS)
                         + dt_bias.astype(jnp.float32))
    a_neg = -jnp.exp(a_log.astype(jnp.float32))
    y_fwd = ssd_chunked(xs, dt[:, :, 0], a_neg[0], bm, cm)
    fl = lambda t: jnp.flip(t, axis=1)
    y_bwd = fl(ssd_chunked(fl(xs), fl(dt[:, :, 1]), a_neg[1], fl(bm), fl(cm)))
    y = y_fwd + y_bwd + d_skip.astype(jnp.float32)[:, None] * xs.astype(jnp.float32)
    u = (y.reshape(bsz, L, D_INNER) * jax.nn.silu(z.astype(jnp.float32))).reshape(bsz, L, N_GROUPS, D_INNER // N_GROUPS)
    u = u * lax.rsqrt(jnp.mean(u * u, axis=-1, keepdims=True) + EPS)
    u = u.reshape(bsz, L, D_INNER) * norm_g.astype(jnp.float32)
    return u.astype(h.dtype) @ w_out


def sq_relu_mlp(h, w1, w2):
    return jnp.square(jax.nn.relu(h @ w1)) @ w2


def setup_inputs(seed: int = 0) -> dict:
    key = jax.random.key(seed)
    ks = jax.random.split(key, 24)
    nrm = lambda k, shape, s: jax.random.normal(k, shape, jnp.float32) * s
    dt0 = jnp.exp(jax.random.uniform(ks[14], (N_SSD_LAYERS, 2, N_SSM_HEADS), jnp.float32)
                  * (np.log(0.1) - np.log(0.001)) + np.log(0.001))
    return {
        "x": nrm(ks[0], (BATCH, SEQ, D_MODEL), 1.0),
        "c": nrm(ks[1], (BATCH, D_MODEL), 1.0),
        "ada_w": nrm(ks[2], (DEPTH, 2, D_MODEL, 3 * D_MODEL), 0.5 * D_MODEL ** -0.5),
        "ada_b": nrm(ks[3], (DEPTH, 2, 3 * D_MODEL), 0.02),
        "norm_g": 1.0 + nrm(ks[4], (DEPTH, 2, D_MODEL), 0.1),
        "final_g": 1.0 + nrm(ks[5], (D_MODEL,), 0.1),
        "sc_in_w": nrm(ks[6], (N_SC_LAYERS, D_MODEL, 3 * D_MODEL), D_MODEL ** -0.5),
        "sc_conv_w": nrm(ks[7], (N_SC_LAYERS, D_MODEL, SC_CONV), SC_CONV ** -0.5),
        "sc_out_w": nrm(ks[8], (N_SC_LAYERS, D_MODEL, D_MODEL), D_MODEL ** -0.5),
        "m2_in_w": nrm(ks[9], (N_SSD_LAYERS, D_MODEL, IN_PROJ_DIM), D_MODEL ** -0.5),
        "m2_conv_w": nrm(ks[10], (N_SSD_LAYERS, CONV_DIM, SSM_CONV), SSM_CONV ** -0.5),
        "m2_conv_b": nrm(ks[11], (N_SSD_LAYERS, CONV_DIM), 0.02),
        "m2_dt_bias": dt0 + jnp.log(-jnp.expm1(-dt0)),
        "m2_a_log": jnp.log(jax.random.uniform(ks[12], (N_SSD_LAYERS, 2, N_SSM_HEADS), jnp.float32, 1.0, 16.0)),
        "m2_d": 1.0 + nrm(ks[13], (N_SSD_LAYERS, N_SSM_HEADS), 0.1),
        "m2_norm_g": 1.0 + nrm(ks[15], (N_SSD_LAYERS, D_INNER), 0.1),
        "m2_out_w": nrm(ks[16], (N_SSD_LAYERS, D_INNER, D_MODEL), D_INNER ** -0.5),
        "mlp_w1": nrm(ks[17], (DEPTH, D_MODEL, D_FF), D_MODEL ** -0.5),
        "mlp_w2": nrm(ks[18], (DEPTH, D_FF, D_MODEL), D_FF ** -0.5),
    }


def reference(x, c, ada_w, ada_b, norm_g, final_g, sc_in_w, sc_conv_w, sc_out_w,
              m2_in_w, m2_conv_w, m2_conv_b, m2_dt_bias, m2_a_log, m2_d, m2_norm_g, m2_out_w,
              mlp_w1, mlp_w2):
    cond = jax.nn.silu(c)

    def modulation(i, s):
        mod = cond @ ada_w[i, s] + ada_b[i, s]
        shift, scale, gate = jnp.split(mod, 3, axis=-1)
        return shift[:, None, :], scale[:, None, :], gate[:, None, :]

    for i in range(DEPTH):
        j = i // N_MIXERS
        shift, scale, gate = modulation(i, 0)
        h = rms_norm(x, norm_g[i, 0]) * (1.0 + scale) + shift
        if i % N_MIXERS == 0:
            out = short_conv_mixer(h, sc_in_w[j], sc_conv_w[j], sc_out_w[j])
        else:
            out = mamba2_bidir_mixer(h, m2_in_w[j], m2_conv_w[j], m2_conv_b[j], m2_dt_bias[j],
                                     m2_a_log[j], m2_d[j], m2_norm_g[j], m2_out_w[j])
        x = x + gate * out
        shift, scale, gate = modulation(i, 1)
        h = rms_norm(x, norm_g[i, 1]) * (1.0 + scale) + shift
        x = x + gate * sq_relu_mlp(h, mlp_w1[i], mlp_w2[i])
    return rms_norm(x, final_g)
```

```python
import functools

import jax
import jax.numpy as jnp
from jax import lax
from jax.experimental import pallas as pl
from jax.experimental.pallas import tpu as pltpu

N_GROUPS = 8
CHUNK = 128
EPS = 1e-6
HALO_ROWS = 16
VMEM_LIMIT_BYTES = 56 * 1024 * 1024

F32 = jnp.float32
BF16 = jnp.bfloat16


def _params(*sem):
    return pltpu.CompilerParams(dimension_semantics=sem, vmem_limit_bytes=VMEM_LIMIT_BYTES)


def _dot(a, b):
    return jnp.dot(a, b, preferred_element_type=F32)


def _silu(x):
    return x * jax.nn.sigmoid(x)


def _rms(x, g):
    ms = jnp.mean(x * x, axis=-1, keepdims=True)
    return (x * lax.rsqrt(ms + EPS)) * g


def _modnorm(x, g, scale, shift):
    return _rms(x, g) * (1.0 + scale) + shift


def _conv3(main, prev_row, next_row, w):
    tm = main.shape[0]
    row = lax.broadcasted_iota(jnp.int32, main.shape, 0)
    up = jnp.where(row == 0, prev_row, pltpu.roll(main, 1, 0))
    dn = jnp.where(row == tm - 1, next_row, pltpu.roll(main, tm - 1, 0))
    return w[0:1, :] * up + w[1:2, :] * main + w[2:3, :] * dn


def _halo_rows(prev_ref, next_ref, tiles_per_seq):
    i = pl.program_id(0) % tiles_per_seq
    prev = prev_ref[...].astype(F32)[HALO_ROWS - 1:HALO_ROWS, :]
    nxt = next_ref[...].astype(F32)[0:1, :]
    prev = jnp.where(i == 0, 0.0, prev)
    nxt = jnp.where(i == tiles_per_seq - 1, 0.0, nxt)
    return prev, nxt


def _mod_kernel(c_ref, w_ref, b_ref, o_ref):
    cond = _silu(c_ref[...])
    o_ref[0] = jnp.dot(cond, w_ref[0], preferred_element_type=F32,
                       precision=lax.Precision.HIGHEST) + b_ref[0]


def _modulation(c, ada_w, ada_b):
    bsz, d = c.shape
    n_mod = ada_w.shape[0] * ada_w.shape[1]
    w = ada_w.reshape(n_mod, d, 3 * d)
    b = ada_b.reshape(n_mod, 1, 3 * d)
    rows = 8
    c_pad = jnp.pad(c, ((0, rows - bsz), (0, 0)))
    tn = 768
    out = pl.pallas_call(
        _mod_kernel,
        grid=(n_mod, 3 * d // tn),
        in_specs=[pl.BlockSpec((rows, d), lambda m, j: (0, 0)),
                  pl.BlockSpec((1, d, tn), lambda m, j: (m, 0, j)),
                  pl.BlockSpec((1, 1, tn), lambda m, j: (m, 0, j))],
        out_specs=pl.BlockSpec((1, rows, tn), lambda m, j: (m, 0, j)),
        out_shape=jax.ShapeDtypeStruct((n_mod, rows, 3 * d), F32),
        compiler_params=_params("arbitrary", "arbitrary"),
        name="modulation",
    )(c_pad, w, b)
    out = out[:, :bsz, :].reshape(n_mod, bsz, 3, 1, d)
    return [(out[m, :, 0], out[m, :, 1], out[m, :, 2]) for m in range(n_mod)]


def _sc_in_kernel(x_ref, g_ref, sc_ref, sh_ref, wb_ref, wc_ref, wv_ref, b_out, cv_out, h_scr):
    @pl.when(pl.program_id(1) == 0)
    def _():
        h_scr[...] = _modnorm(x_ref[...], g_ref[...], sc_ref[0], sh_ref[0]).astype(BF16)

    h = h_scr[...]
    b_out[...] = _dot(h, wb_ref[...]).astype(BF16)
    cv_out[...] = (_dot(h, wc_ref[...]) * _dot(h, wv_ref[...])).astype(BF16)


def _sc_in(x2, g, scale, shift, w_in, seq, tm, tn):
    t, d = x2.shape
    tps = seq // tm
    nb = d // tn
    row_spec = pl.BlockSpec((tm, d), lambda i, j: (i, 0))
    vec_spec = pl.BlockSpec((1, d), lambda i, j: (0, 0))
    mod_spec = pl.BlockSpec((1, 1, d), lambda i, j: (i // tps, 0, 0))
    out_spec = pl.BlockSpec((tm, tn), lambda i, j: (i, j))
    return pl.pallas_call(
        _sc_in_kernel,
        grid=(t // tm, nb),
        in_specs=[row_spec, vec_spec, mod_spec, mod_spec,
                  pl.BlockSpec((d, tn), lambda i, j: (0, j)),
                  pl.BlockSpec((d, tn), lambda i, j: (0, j + nb)),
                  pl.BlockSpec((d, tn), lambda i, j: (0, j + 2 * nb))],
        out_specs=[out_spec, out_spec],
        out_shape=[jax.ShapeDtypeStruct((t, d), BF16), jax.ShapeDtypeStruct((t, d), BF16)],
        scratch_shapes=[pltpu.VMEM((tm, d), BF16)],
        compiler_params=_params("arbitrary", "arbitrary"),
        name="sc_in_proj",
    )(x2, g, scale, shift, w_in, w_in, w_in)


def _sc_out_kernel(cv_ref, cvp_ref, cvn_ref, bg_ref, cw_ref, w_ref, x_ref, gate_ref, o_ref, *, tiles_per_seq):
    prev, nxt = _halo_rows(cvp_ref, cvn_ref, tiles_per_seq)
    u = _conv3(cv_ref[...].astype(F32), prev, nxt, cw_ref[...])
    u = (u * bg_ref[...].astype(F32)).astype(BF16)
    o_ref[...] = x_ref[...] + gate_ref[0] * _dot(u, w_ref[...])


def _halo_specs(tm, width, t, col_off=0):
    hb = tm // HALO_ROWS
    last = t // HALO_ROWS - 1
    prev = pl.BlockSpec((HALO_ROWS, width), lambda i: (jnp.maximum(i * hb - 1, 0), col_off))
    nxt = pl.BlockSpec((HALO_ROWS, width), lambda i: (jnp.minimum((i + 1) * hb, last), col_off))
    return prev, nxt


def _sc_out(cv, bg, conv_w, w_out, x2, gate, seq, tm):
    t, d = x2.shape
    tps = seq // tm
    row_spec = pl.BlockSpec((tm, d), lambda i: (i, 0))
    prev_spec, next_spec = _halo_specs(tm, d, t)
    return pl.pallas_call(
        functools.partial(_sc_out_kernel, tiles_per_seq=tps),
        grid=(t // tm,),
        in_specs=[row_spec, prev_spec, next_spec, row_spec,
                  pl.BlockSpec((3, d), lambda i: (0, 0)),
                  pl.BlockSpec((d, d), lambda i: (0, 0)),
                  row_spec,
                  pl.BlockSpec((1, 1, d), lambda i: (i // tps, 0, 0))],
        out_specs=row_spec,
        out_shape=jax.ShapeDtypeStruct((t, d), F32),
        compiler_params=_params("arbitrary"),
        name="sc_conv_out_proj",
    )(cv, cv, cv, bg, conv_w, w_out, x2, gate)


def _mlp_kernel(x_ref, g_ref, sc_ref, sh_ref, gate_ref, w1_ref, w2_ref, fg_ref, o_ref, h_scr, *, final_norm):
    k = pl.program_id(1)

    @pl.when(k == 0)
    def _():
        h_scr[...] = _modnorm(x_ref[...], g_ref[...], sc_ref[0], sh_ref[0]).astype(BF16)
        o_ref[...] = jnp.zeros_like(o_ref)

    a = jnp.maximum(_dot(h_scr[...], w1_ref[...]), 0.0)
    o_ref[...] += _dot((a * a).astype(BF16), w2_ref[...])

    @pl.when(k == pl.num_programs(1) - 1)
    def _():
        y = x_ref[...] + gate_ref[0] * o_ref[...]
        if final_norm:
            y = _rms(y, fg_ref[...])
        o_ref[...] = y


def _mlp(x2, g, scale, shift, gate, w1, w2, final_g, seq, tm, tf, final_norm):
    t, d = x2.shape
    ff = w1.shape[1]
    tps = seq // tm
    row_spec = pl.BlockSpec((tm, d), lambda i, k: (i, 0))
    vec_spec = pl.BlockSpec((1, d), lambda i, k: (0, 0))
    mod_spec = pl.BlockSpec((1, 1, d), lambda i, k: (i // tps, 0, 0))
    return pl.pallas_call(
        functools.partial(_mlp_kernel, final_norm=final_norm),
        grid=(t // tm, ff // tf),
        in_specs=[row_spec, vec_spec, mod_spec, mod_spec, mod_spec,
                  pl.BlockSpec((d, tf), lambda i, k: (0, k)),
                  pl.BlockSpec((tf, d), lambda i, k: (k, 0)),
                  vec_spec],
        out_specs=row_spec,
        out_shape=jax.ShapeDtypeStruct((t, d), F32),
        scratch_shapes=[pltpu.VMEM((tm, d), BF16)],
        compiler_params=_params("arbitrary", "arbitrary"),
        name="mlp_final" if final_norm else "mlp",
    )(x2, g, scale, shift, gate, w1, w2, final_g)


def _m2_in_kernel(x_ref, g_ref, sc_ref, sh_ref, w_ref, wdt_ref, o_ref, dt_ref, h_scr):
    @pl.when(pl.program_id(1) == 0)
    def _():
        h = _modnorm(x_ref[...], g_ref[...], sc_ref[0], sh_ref[0]).astype(BF16)
        h_scr[...] = h
        dt_ref[...] = _dot(h, wdt_ref[...])

    o_ref[...] = _dot(h_scr[...], w_ref[...]).astype(BF16)


def _m2_in(x2, g, scale, shift, w_main, w_dt, seq, tm, tn):
    t, d = x2.shape
    n = w_main.shape[1]
    ndt = w_dt.shape[1]
    tps = seq // tm
    return pl.pallas_call(
        _m2_in_kernel,
        grid=(t // tm, n // tn),
        in_specs=[pl.BlockSpec((tm, d), lambda i, j: (i, 0)),
                  pl.BlockSpec((1, d), lambda i, j: (0, 0)),
                  pl.BlockSpec((1, 1, d), lambda i, j: (i // tps, 0, 0)),
                  pl.BlockSpec((1, 1, d), lambda i, j: (i // tps, 0, 0)),
                  pl.BlockSpec((d, tn), lambda i, j: (0, j)),
                  pl.BlockSpec((d, ndt), lambda i, j: (0, 0))],
        out_specs=[pl.BlockSpec((tm, tn), lambda i, j: (i, j)),
                   pl.BlockSpec((tm, ndt), lambda i, j: (i, 0))],
        out_shape=[jax.ShapeDtypeStruct((t, n), BF16), jax.ShapeDtypeStruct((t, ndt), F32)],
        scratch_shapes=[pltpu.VMEM((tm, d), BF16)],
        compiler_params=_params("arbitrary", "arbitrary"),
        name="m2_in_proj",
    )(x2, g, scale, shift, w_main, w_dt)


def _conv_silu_kernel(m_ref, p_ref, n_ref, w_ref, b_ref, o_ref, *, tiles_per_seq):
    prev, nxt = _halo_rows(p_ref, n_ref, tiles_per_seq)
    u = _conv3(m_ref[...].astype(F32), prev, nxt, w_ref[...]) + b_ref[...]
    o_ref[...] = _silu(u).astype(BF16)


def _conv_silu(zxbc, conv_w, conv_b, d_inner, seq, tm, tn):
    t = zxbc.shape[0]
    cdim = conv_w.shape[1]
    tps = seq // tm
    off = d_inner // tn
    hb = tm // HALO_ROWS
    last = t // HALO_ROWS - 1
    return pl.pallas_call(
        functools.partial(_conv_silu_kernel, tiles_per_seq=tps),
        grid=(t // tm, cdim // tn),
        in_specs=[pl.BlockSpec((tm, tn), lambda i, j: (i, j + off)),
                  pl.BlockSpec((HALO_ROWS, tn), lambda i, j: (jnp.maximum(i * hb - 1, 0), j + off)),
                  pl.BlockSpec((HALO_ROWS, tn), lambda i, j: (jnp.minimum((i + 1) * hb, last), j + off)),
                  pl.BlockSpec((3, tn), lambda i, j: (0, j)),
                  pl.BlockSpec((1, tn), lambda i, j: (0, j))],
        out_specs=pl.BlockSpec((tm, tn), lambda i, j: (i, j)),
        out_shape=jax.ShapeDtypeStruct((t, cdim), BF16),
        compiler_params=_params("arbitrary", "arbitrary"),
        name="m2_conv_silu",
    )(zxbc, zxbc, zxbc, conv_w, conv_b)


def _dt_kernel(raw_ref, bias_ref, alog_ref, cum_ref, ecum_ref, wend_ref, cumt_ref, dtt_ref, elast_ref, *, n_heads):
    q = raw_ref.shape[0]
    v = raw_ref[...] + bias_ref[...]
    dt = jnp.maximum(v, 0.0) + jnp.log1p(jnp.exp(-jnp.abs(v)))
    a = dt * (-jnp.exp(alog_ref[...]))
    r = lax.broadcasted_iota(jnp.int32, (q, q), 0)
    c = lax.broadcasted_iota(jnp.int32, (q, q), 1)
    hi = lax.Precision.HIGHEST
    cum_f = jnp.dot((c <= r).astype(F32), a, preferred_element_type=F32, precision=hi)
    cum_b = jnp.dot((c >= r).astype(F32), a, preferred_element_type=F32, precision=hi)
    fwd = lax.broadcasted_iota(jnp.int32, a.shape, 1) < n_heads
    cum = jnp.where(fwd, cum_f, cum_b)
    last = jnp.where(fwd[0:1, :], cum[q - 1:q, :], cum[0:1, :])
    cum_ref[...] = cum
    ecum_ref[...] = jnp.exp(cum)
    wend_ref[...] = jnp.exp(last - cum) * dt
    cumt_ref[0] = cum.T
    dtt_ref[0] = dt.T
    elast_ref[0] = jnp.exp(last)


def _dt_prep(dt_raw, dt_bias, a_log, bsz, seq):
    t, w = dt_raw.shape
    q = CHUNK
    nc = seq // q
    col_spec = pl.BlockSpec((q, w), lambda i: (i, 0))
    vec_spec = pl.BlockSpec((1, w), lambda i: (0, 0))
    row_spec = pl.BlockSpec((1, w, q), lambda i: (i // nc, 0, i % nc))
    col_shape = jax.ShapeDtypeStruct((t, w), F32)
    row_shape = jax.ShapeDtypeStruct((bsz, w, seq), F32)
    return pl.pallas_call(
        functools.partial(_dt_kernel, n_heads=w // 2),
        grid=(t // q,),
        in_specs=[col_spec, vec_spec, vec_spec],
        out_specs=[col_spec, col_spec, col_spec, row_spec, row_spec,
                   pl.BlockSpec((1, 1, w), lambda i: (i, 0, 0))],
        out_shape=[col_shape, col_shape, col_shape, row_shape, row_shape,
                   jax.ShapeDtypeStruct((t // q, 1, w), F32)],
        compiler_params=_params("arbitrary"),
        name="m2_dt_prep",
    )(dt_raw, dt_bias, a_log)


def _expand_heads(col, lo_half):
    q, r = col.shape
    parts = []
    for k in range(r // 2):
        a = jnp.broadcast_to(col[:, 2 * k:2 * k + 1], (q, 128))
        b = jnp.broadcast_to(col[:, 2 * k + 1:2 * k + 2], (q, 128))
        parts.append(jnp.where(lo_half, a, b))
    return jnp.concatenate(parts, axis=1)


def _ssd_kernel(xf_ref, bf_ref, cf_ref, xb_ref, bb_ref, cb_ref,
                cumc_ref, ecf_ref, ecb_ref, wef_ref, web_ref,
                ctf_ref, ctb_ref, dtf_ref, dtb_ref, elf_ref, elb_ref, dsk_ref,
                yf_ref, yb_ref, stf, stb, *, heads_per_group, head_dim):
    rr = heads_per_group
    q = xf_ref.shape[0]

    @pl.when(pl.program_id(2) == 0)
    def _():
        stf[...] = jnp.zeros_like(stf)
        stb[...] = jnp.zeros_like(stb)

    lo_half = lax.broadcasted_iota(jnp.int32, (q, 128), 1) < head_dim
    nt = (((1,), (1,)), ((), ()))

    x = xf_ref[...]
    bm = bf_ref[...]
    cm = cf_ref[...]
    scores = lax.dot_general(cm, bm, nt, preferred_element_type=F32)
    t_idx = lax.broadcasted_iota(jnp.int32, (q, q), 0)
    s_idx = lax.broadcasted_iota(jnp.int32, (q, q), 1)
    lower = t_idx >= s_idx
    upper = t_idx <= s_idx
    cumc = cumc_ref[0, 0]
    ctf, ctb, dtf, dtb = ctf_ref[0], ctb_ref[0], dtf_ref[0], dtb_ref[0]
    neg_inf = -jnp.inf
    heads_per_tile = 256 // head_dim
    lane_head = lax.broadcasted_iota(jnp.int32, (q, 256), 1) // head_dim
    y_parts = []
    for tile in range(rr * head_dim // 256):
        xt = x[:, tile * 256:(tile + 1) * 256]
        acc = None
        for pair in range(heads_per_tile // 2):
            ms, xs = [], []
            for j in (2 * pair, 2 * pair + 1):
                h = tile * heads_per_tile + j
                seg_f = jnp.where(lower, cumc[:, h:h + 1] - ctf[h:h + 1, :], neg_inf)
                seg_b = jnp.where(upper, cumc[:, rr + h:rr + h + 1] - ctb[h:h + 1, :], neg_inf)
                w = jnp.exp(seg_f) * dtf[h:h + 1, :] + jnp.exp(seg_b) * dtb[h:h + 1, :]
                ms.append((scores * w).astype(BF16))
                xs.append(jnp.where(lane_head == j, xt, jnp.zeros_like(xt)))
            part = _dot(jnp.concatenate(ms, axis=1), jnp.concatenate(xs, axis=0))
            acc = part if acc is None else acc + part
        y_parts.append(acc)
    y = jnp.concatenate(y_parts, axis=1)

    xf32 = x.astype(F32)
    y = y + _expand_heads(ecf_ref[0, 0][:, :rr], lo_half) * _dot(cm, stf[...].astype(BF16))
    y = y + dsk_ref[...] * xf32
    yf_ref[...] = y.astype(BF16)
    xw = (xf32 * _expand_heads(wef_ref[0, 0][:, :rr], lo_half)).astype(BF16)
    bt = bm.astype(F32).T.astype(BF16)
    stf[...] = stf[...] * elf_ref[0] + _dot(bt, xw)

    xb = xb_ref[...].astype(F32)
    bmb = bb_ref[...]
    yb = _expand_heads(ecb_ref[0, 0][:, rr:], lo_half) * _dot(cb_ref[...], stb[...].astype(BF16))
    yb_ref[...] = yb.astype(BF16)
    xwb = (xb * _expand_heads(web_ref[0, 0][:, rr:], lo_half)).astype(BF16)
    btb = bmb.astype(F32).T.astype(BF16)
    stb[...] = stb[...] * elb_ref[0] + _dot(btb, xwb)


def _ssd(xbc, cum_c, ecum_c, wend_c, cum_t, dt_t, elast, dskip, bsz, seq, d_inner, d_state, n_heads):
    t = xbc.shape[0]
    q = CHUNK
    nc = seq // q
    g = N_GROUPS
    rr = n_heads // g
    p = d_inner // n_heads
    gw = rr * p
    assert p == 64 and d_state == 128 and gw % 256 == 0 and rr % 2 == 0
    boff = d_inner // d_state

    def fw(b, gi, c):
        return b * nc + c

    def bw(b, gi, c):
        return b * nc + (nc - 1 - c)

    x_spec = lambda row: pl.BlockSpec((q, gw), lambda b, gi, c: (row(b, gi, c), gi))
    b_spec = lambda row: pl.BlockSpec((q, d_state), lambda b, gi, c: (row(b, gi, c), boff + gi))
    c_spec = lambda row: pl.BlockSpec((q, d_state), lambda b, gi, c: (row(b, gi, c), boff + g + gi))
    colf = pl.BlockSpec((1, 1, q, 2 * rr), lambda b, gi, c: (b, gi, c, 0))
    colb = pl.BlockSpec((1, 1, q, 2 * rr), lambda b, gi, c: (b, gi, nc - 1 - c, 0))
    rowf = pl.BlockSpec((1, rr, q), lambda b, gi, c: (b, gi, c))
    rowb = pl.BlockSpec((1, rr, q), lambda b, gi, c: (b, g + gi, c))
    elf = pl.BlockSpec((1, 1, gw), lambda b, gi, c: (2 * fw(b, gi, c), 0, gi))
    elb = pl.BlockSpec((1, 1, gw), lambda b, gi, c: (2 * bw(b, gi, c) + 1, 0, gi))
    y_shape = jax.ShapeDtypeStruct((t, d_inner), BF16)
    return pl.pallas_call(
        functools.partial(_ssd_kernel, heads_per_group=rr, head_dim=p),
        grid=(bsz, g, nc),
        in_specs=[x_spec(fw), b_spec(fw), c_spec(fw), x_spec(bw), b_spec(bw), c_spec(bw),
                  colf, colf, colb, colf, colb,
                  rowf, rowb, rowf, rowb, elf, elb,
                  pl.BlockSpec((1, gw), lambda b, gi, c: (0, gi))],
        out_specs=[pl.BlockSpec((q, gw), lambda b, gi, c: (fw(b, gi, c), gi)),
                   pl.BlockSpec((q, gw), lambda b, gi, c: (bw(b, gi, c), gi))],
        out_shape=[y_shape, y_shape],
        scratch_shapes=[pltpu.VMEM((d_state, gw), F32), pltpu.VMEM((d_state, gw), F32)],
        compiler_params=_params("arbitrary", "arbitrary", "arbitrary"),
        name="m2_ssd",
    )(xbc, xbc, xbc, xbc, xbc, xbc,
      cum_c, ecum_c, ecum_c, wend_c, wend_c,
      cum_t, cum_t, dt_t, dt_t, elast, elast, dskip)


def _m2_out_kernel(yf_ref, yb_ref, z_ref, ng_ref, w_ref, x_ref, gate_ref, o_ref, u_scr, *, n_groups):
    @pl.when(pl.program_id(1) == 0)
    def _():
        z = z_ref[...].astype(F32)
        u = (yf_ref[...].astype(F32) + yb_ref[...].astype(F32)) * _silu(z)
        gs = u.shape[1] // n_groups
        for k in range(n_groups):
            sl = slice(k * gs, (k + 1) * gs)
            u_scr[:, sl] = _rms(u[:, sl], ng_ref[:, sl]).astype(BF16)

    o_ref[...] = x_ref[...] + gate_ref[0] * _dot(u_scr[...], w_ref[...])


def _m2_out(yf, yb, zxbc, norm_g, w_out, x2, gate, seq, tm, tn):
    t, d = x2.shape
    di = yf.shape[1]
    tps = seq // tm
    wide = pl.BlockSpec((tm, di), lambda i, j: (i, 0))
    return pl.pallas_call(
        functools.partial(_m2_out_kernel, n_groups=N_GROUPS),
        grid=(t // tm, d // tn),
        in_specs=[wide, wide, wide,
                  pl.BlockSpec((1, di), lambda i, j: (0, 0)),
                  pl.BlockSpec((di, tn), lambda i, j: (0, j)),
                  pl.BlockSpec((tm, tn), lambda i, j: (i, j)),
                  pl.BlockSpec((1, 1, tn), lambda i, j: (i // tps, 0, j))],
        out_specs=pl.BlockSpec((tm, tn), lambda i, j: (i, j)),
        out_shape=jax.ShapeDtypeStruct((t, d), F32),
        scratch_shapes=[pltpu.VMEM((tm, di), BF16)],
        compiler_params=_params("arbitrary", "arbitrary"),
        name="m2_norm_out_proj",
    )(yf, yb, zxbc, norm_g, w_out, x2, gate)


def _row_tile(seq, want):
    tm = min(want, seq)
    assert seq % tm == 0 and tm % HALO_ROWS == 0
    return tm


def kernel(x, c, ada_w, ada_b, norm_g, final_g, sc_in_w, sc_conv_w, sc_out_w, m2_in_w, m2_conv_w, m2_conv_b,
           m2_dt_bias, m2_a_log, m2_d, m2_norm_g, m2_out_w, mlp_w1, mlp_w2):
    bsz, seq, d = x.shape
    t = bsz * seq
    d_inner = m2_norm_g.shape[1]
    n_heads = m2_d.shape[1]
    conv_dim = m2_conv_w.shape[1]
    d_state = (conv_dim - d_inner) // (2 * N_GROUPS)
    head_dim = d_inner // n_heads
    assert seq % CHUNK == 0
    tm = _row_tile(seq, 512)

    mods = _modulation(c, ada_w, ada_b)
    x2 = x.reshape(t, d)
    vec = lambda v: v.reshape(1, -1)

    shift, scale, gate = mods[0]
    bg, cv = _sc_in(x2, vec(norm_g[0, 0]), scale, shift, sc_in_w[0].astype(BF16), seq, tm, 512)
    x2 = _sc_out(cv, bg, sc_conv_w[0].T, sc_out_w[0].astype(BF16), x2, gate, seq, tm)
    shift, scale, gate = mods[1]
    x2 = _mlp(x2, vec(norm_g[0, 1]), scale, shift, gate, mlp_w1[0].astype(BF16), mlp_w2[0].astype(BF16),
              vec(final_g), seq, tm, 512, final_norm=False)

    shift, scale, gate = mods[2]
    n_main = d_inner + conv_dim
    w_in = m2_in_w[0].astype(BF16)
    zxbc, dt_raw = _m2_in(x2, vec(norm_g[1, 0]), scale, shift, w_in[:, :n_main], w_in[:, n_main:], seq, tm, 1024)
    xbc = _conv_silu(zxbc, m2_conv_w[0].T, vec(m2_conv_b[0]), d_inner, seq, tm, 1024)
    cum, ecum, wend, cum_t, dt_t, elast = _dt_prep(dt_raw, vec(m2_dt_bias[0]), vec(m2_a_log[0]), bsz, seq)
    rr = n_heads // N_GROUPS

    def by_group(a):
        a = a.reshape(bsz, seq, 2, N_GROUPS, rr)
        return jnp.transpose(a, (0, 3, 1, 2, 4)).reshape(bsz, N_GROUPS, seq, 2 * rr)

    elast = jnp.repeat(elast.reshape(-1, 2, n_heads), head_dim, axis=2).reshape(-1, 1, d_inner)
    dskip = jnp.repeat(m2_d[0], head_dim).reshape(1, d_inner)
    yf, yb = _ssd(xbc, by_group(cum), by_group(ecum), by_group(wend), cum_t, dt_t, elast, dskip,
                  bsz, seq, d_inner, d_state, n_heads)
    x2 = _m2_out(yf, yb, zxbc, vec(m2_norm_g[0]), m2_out_w[0].astype(BF16), x2, gate, seq, tm, 512)
    shift, scale, gate = mods[3]
    x2 = _mlp(x2, vec(norm_g[1, 1]), scale, shift, gate, mlp_w1[1].astype(BF16), mlp_w2[1].astype(BF16),
              vec(final_g), seq, tm, 512, final_norm=True)
    return x2.reshape(bsz, seq, d)
```

```python
import functools

import numpy as np
import jax
import jax.numpy as jnp
from jax import lax
from jax.experimental import pallas as pl
from jax.experimental.pallas import tpu as pltpu

N_GROUPS = 8
CHUNK = 128
SSD_CHUNKS_PER_STEP = 4
EPS = 1e-6
LOG2_E = 1.4426950408889634
HALO_ROWS = 16
VMEM_LIMIT_BYTES = 56 * 1024 * 1024

F32 = jnp.float32
BF16 = jnp.bfloat16


def _params(*sem):
    return pltpu.CompilerParams(dimension_semantics=sem, vmem_limit_bytes=VMEM_LIMIT_BYTES)


def _dot(a, b):
    return jnp.dot(a, b, preferred_element_type=F32)


def _silu(x):
    return x * jax.nn.sigmoid(x)


def _rms(x, g):
    ms = jnp.mean(x * x, axis=-1, keepdims=True)
    return (x * lax.rsqrt(ms + EPS)) * g


def _modnorm(x, g, scale, shift):
    return _rms(x, g) * (1.0 + scale) + shift


def _conv3(main, prev_row, next_row, w):
    tm = main.shape[0]
    row = lax.broadcasted_iota(jnp.int32, main.shape, 0)
    up = jnp.where(row == 0, prev_row, pltpu.roll(main, 1, 0))
    dn = jnp.where(row == tm - 1, next_row, pltpu.roll(main, tm - 1, 0))
    return w[0:1, :] * up + w[1:2, :] * main + w[2:3, :] * dn


def _halo_rows(prev_ref, next_ref, tiles_per_seq):
    i = pl.program_id(0) % tiles_per_seq
    prev = prev_ref[...].astype(F32)[HALO_ROWS - 1:HALO_ROWS, :]
    nxt = next_ref[...].astype(F32)[0:1, :]
    prev = jnp.where(i == 0, 0.0, prev)
    nxt = jnp.where(i == tiles_per_seq - 1, 0.0, nxt)
    return prev, nxt


def _mod_kernel(c_ref, w_ref, b_ref, o_ref):
    cond = _silu(c_ref[...])
    o_ref[0] = jnp.dot(cond, w_ref[0], preferred_element_type=F32,
                       precision=lax.Precision.HIGHEST) + b_ref[0]


def _modulation(c, ada_w, ada_b):
    bsz, d = c.shape
    n_mod = ada_w.shape[0] * ada_w.shape[1]
    w = ada_w.reshape(n_mod, d, 3 * d)
    b = ada_b.reshape(n_mod, 1, 3 * d)
    rows = 8
    c_pad = jnp.pad(c, ((0, rows - bsz), (0, 0)))
    tn = 768
    out = pl.pallas_call(
        _mod_kernel,
        grid=(n_mod, 3 * d // tn),
        in_specs=[pl.BlockSpec((rows, d), lambda m, j: (0, 0)),
                  pl.BlockSpec((1, d, tn), lambda m, j: (m, 0, j)),
                  pl.BlockSpec((1, 1, tn), lambda m, j: (m, 0, j))],
        out_specs=pl.BlockSpec((1, rows, tn), lambda m, j: (m, 0, j)),
        out_shape=jax.ShapeDtypeStruct((n_mod, rows, 3 * d), F32),
        compiler_params=_params("arbitrary", "arbitrary"),
        name="modulation",
    )(c_pad, w, b)
    out = out[:, :bsz, :].reshape(n_mod, bsz, 3, 1, d)
    return [(out[m, :, 0], out[m, :, 1], out[m, :, 2]) for m in range(n_mod)]


def _sc_in_kernel(x_ref, g_ref, sc_ref, sh_ref, wb_ref, wc_ref, wv_ref, b_out, cv_out, h_scr):
    @pl.when(pl.program_id(1) == 0)
    def _():
        h_scr[...] = _modnorm(x_ref[...], g_ref[...], sc_ref[0], sh_ref[0]).astype(BF16)

    h = h_scr[...]
    b_out[...] = _dot(h, wb_ref[...]).astype(BF16)
    cv_out[...] = (_dot(h, wc_ref[...]) * _dot(h, wv_ref[...])).astype(BF16)


def _sc_in(x2, g, scale, shift, w_in, seq, tm, tn):
    t, d = x2.shape
    tps = seq // tm
    nb = d // tn
    row_spec = pl.BlockSpec((tm, d), lambda i, j: (i, 0))
    vec_spec = pl.BlockSpec((1, d), lambda i, j: (0, 0))
    mod_spec = pl.BlockSpec((1, 1, d), lambda i, j: (i // tps, 0, 0))
    out_spec = pl.BlockSpec((tm, tn), lambda i, j: (i, j))
    return pl.pallas_call(
        _sc_in_kernel,
        grid=(t // tm, nb),
        in_specs=[row_spec, vec_spec, mod_spec, mod_spec,
                  pl.BlockSpec((d, tn), lambda i, j: (0, j)),
                  pl.BlockSpec((d, tn), lambda i, j: (0, j + nb)),
                  pl.BlockSpec((d, tn), lambda i, j: (0, j + 2 * nb))],
        out_specs=[out_spec, out_spec],
        out_shape=[jax.ShapeDtypeStruct((t, d), BF16), jax.ShapeDtypeStruct((t, d), BF16)],
        scratch_shapes=[pltpu.VMEM((tm, d), BF16)],
        compiler_params=_params("arbitrary", "arbitrary"),
        name="sc_in_proj",
    )(x2, g, scale, shift, w_in, w_in, w_in)


def _sc_out_kernel(cv_ref, cvp_ref, cvn_ref, bg_ref, cw_ref, w_ref, x_ref, gate_ref, o_ref, *, tiles_per_seq):
    prev, nxt = _halo_rows(cvp_ref, cvn_ref, tiles_per_seq)
    u = _conv3(cv_ref[...].astype(F32), prev, nxt, cw_ref[...])
    u = (u * bg_ref[...].astype(F32)).astype(BF16)
    o_ref[...] = x_ref[...] + gate_ref[0] * _dot(u, w_ref[...])


def _halo_specs(tm, width, t, col_off=0):
    hb = tm // HALO_ROWS
    last = t // HALO_ROWS - 1
    prev = pl.BlockSpec((HALO_ROWS, width), lambda i: (jnp.maximum(i * hb - 1, 0), col_off))
    nxt = pl.BlockSpec((HALO_ROWS, width), lambda i: (jnp.minimum((i + 1) * hb, last), col_off))
    return prev, nxt


def _sc_out(cv, bg, conv_w, w_out, x2, gate, seq, tm):
    t, d = x2.shape
    tps = seq // tm
    row_spec = pl.BlockSpec((tm, d), lambda i: (i, 0))
    prev_spec, next_spec = _halo_specs(tm, d, t)
    return pl.pallas_call(
        functools.partial(_sc_out_kernel, tiles_per_seq=tps),
        grid=(t // tm,),
        in_specs=[row_spec, prev_spec, next_spec, row_spec,
                  pl.BlockSpec((3, d), lambda i: (0, 0)),
                  pl.BlockSpec((d, d), lambda i: (0, 0)),
                  row_spec,
                  pl.BlockSpec((1, 1, d), lambda i: (i // tps, 0, 0))],
        out_specs=row_spec,
        out_shape=jax.ShapeDtypeStruct((t, d), F32),
        compiler_params=_params("arbitrary"),
        name="sc_conv_out_proj",
    )(cv, cv, cv, bg, conv_w, w_out, x2, gate)


def _mlp_kernel(x_ref, g_ref, sc_ref, sh_ref, gate_ref, w1_ref, w2_ref, fg_ref, o_ref, h_scr, *, final_norm):
    k = pl.program_id(1)

    @pl.when(k == 0)
    def _():
        h_scr[...] = _modnorm(x_ref[...], g_ref[...], sc_ref[0], sh_ref[0]).astype(BF16)
        o_ref[...] = jnp.zeros_like(o_ref)

    a = jnp.maximum(_dot(h_scr[...], w1_ref[...]), 0.0)
    o_ref[...] += _dot((a * a).astype(BF16), w2_ref[...])

    @pl.when(k == pl.num_programs(1) - 1)
    def _():
        y = x_ref[...] + gate_ref[0] * o_ref[...]
        if final_norm:
            y = _rms(y, fg_ref[...])
        o_ref[...] = y


def _mlp(x2, g, scale, shift, gate, w1, w2, final_g, seq, tm, tf, final_norm):
    t, d = x2.shape
    ff = w1.shape[1]
    tps = seq // tm
    row_spec = pl.BlockSpec((tm, d), lambda i, k: (i, 0))
    vec_spec = pl.BlockSpec((1, d), lambda i, k: (0, 0))
    mod_spec = pl.BlockSpec((1, 1, d), lambda i, k: (i // tps, 0, 0))
    return pl.pallas_call(
        functools.partial(_mlp_kernel, final_norm=final_norm),
        grid=(t // tm, ff // tf),
        in_specs=[row_spec, vec_spec, mod_spec, mod_spec, mod_spec,
                  pl.BlockSpec((d, tf), lambda i, k: (0, k)),
                  pl.BlockSpec((tf, d), lambda i, k: (k, 0)),
                  vec_spec],
        out_specs=row_spec,
        out_shape=jax.ShapeDtypeStruct((t, d), F32),
        scratch_shapes=[pltpu.VMEM((tm, d), BF16)],
        compiler_params=_params("arbitrary", "arbitrary"),
        name="mlp_final" if final_norm else "mlp",
    )(x2, g, scale, shift, gate, w1, w2, final_g)


def _m2_in_kernel(x_ref, g_ref, sc_ref, sh_ref, w_ref, wdt_ref, o_ref, dt_ref, h_scr):
    @pl.when(pl.program_id(1) == 0)
    def _():
        h = _modnorm(x_ref[...], g_ref[...], sc_ref[0], sh_ref[0]).astype(BF16)
        h_scr[...] = h
        dt_ref[...] = _dot(h, wdt_ref[...])

    o_ref[...] = _dot(h_scr[...], w_ref[...]).astype(BF16)


def _m2_in(x2, g, scale, shift, w_main, w_dt, seq, tm, tn):
    t, d = x2.shape
    n = w_main.shape[1]
    ndt = w_dt.shape[1]
    tps = seq // tm
    return pl.pallas_call(
        _m2_in_kernel,
        grid=(t // tm, n // tn),
        in_specs=[pl.BlockSpec((tm, d), lambda i, j: (i, 0)),
                  pl.BlockSpec((1, d), lambda i, j: (0, 0)),
                  pl.BlockSpec((1, 1, d), lambda i, j: (i // tps, 0, 0)),
                  pl.BlockSpec((1, 1, d), lambda i, j: (i // tps, 0, 0)),
                  pl.BlockSpec((d, tn), lambda i, j: (0, j)),
                  pl.BlockSpec((d, ndt), lambda i, j: (0, 0))],
        out_specs=[pl.BlockSpec((tm, tn), lambda i, j: (i, j)),
                   pl.BlockSpec((tm, ndt), lambda i, j: (i, 0))],
        out_shape=[jax.ShapeDtypeStruct((t, n), BF16), jax.ShapeDtypeStruct((t, ndt), F32)],
        scratch_shapes=[pltpu.VMEM((tm, d), BF16)],
        compiler_params=_params("arbitrary", "arbitrary"),
        name="m2_in_proj",
    )(x2, g, scale, shift, w_main, w_dt)


def _conv_silu_kernel(m_ref, p_ref, n_ref, w_ref, b_ref, o_ref, *, tiles_per_seq):
    prev, nxt = _halo_rows(p_ref, n_ref, tiles_per_seq)
    u = _conv3(m_ref[...].astype(F32), prev, nxt, w_ref[...]) + b_ref[...]
    o_ref[...] = _silu(u).astype(BF16)


def _conv_silu(zxbc, conv_w, conv_b, d_inner, seq, tm, tn):
    t = zxbc.shape[0]
    cdim = conv_w.shape[1]
    tps = seq // tm
    off = d_inner // tn
    hb = tm // HALO_ROWS
    last = t // HALO_ROWS - 1
    return pl.pallas_call(
        functools.partial(_conv_silu_kernel, tiles_per_seq=tps),
        grid=(t // tm, cdim // tn),
        in_specs=[pl.BlockSpec((tm, tn), lambda i, j: (i, j + off)),
                  pl.BlockSpec((HALO_ROWS, tn), lambda i, j: (jnp.maximum(i * hb - 1, 0), j + off)),
                  pl.BlockSpec((HALO_ROWS, tn), lambda i, j: (jnp.minimum((i + 1) * hb, last), j + off)),
                  pl.BlockSpec((3, tn), lambda i, j: (0, j)),
                  pl.BlockSpec((1, tn), lambda i, j: (0, j))],
        out_specs=pl.BlockSpec((tm, tn), lambda i, j: (i, j)),
        out_shape=jax.ShapeDtypeStruct((t, cdim), BF16),
        compiler_params=_params("arbitrary", "arbitrary"),
        name="m2_conv_silu",
    )(zxbc, zxbc, zxbc, conv_w, conv_b)


N_PIECES = 3
GROUP_LANES = 128


def _piece_base(heads_per_group):
    w = N_PIECES * heads_per_group
    return {"cum_f": 0, "cum_b": w, "ecum_f": 2 * w, "ecum_b": 3 * w, "wend_f": 4 * w}


def _perm_matrices(n_heads, heads_per_group):
    rr = heads_per_group
    base = _piece_base(rr)
    nat = 2 * n_heads
    perm_a = np.zeros((3 * N_PIECES * nat, N_GROUPS * GROUP_LANES), np.float32)
    perm_w = np.zeros((N_PIECES * nat, N_GROUPS * GROUP_LANES), np.float32)
    for piece in range(N_PIECES):
        for d in range(2):
            for h in range(n_heads):
                g, r = divmod(h, rr)
                lane = piece * nat + d * n_heads + h
                col = g * GROUP_LANES + piece * rr + r
                perm_a[0 * N_PIECES * nat + lane, col + (base["cum_f"] if d == 0 else base["cum_b"])] = 1
                perm_a[1 * N_PIECES * nat + lane, col + (base["ecum_f"] if d == 0 else base["ecum_b"])] = 1
                if d == 0:
                    perm_a[2 * N_PIECES * nat + lane, col + base["wend_f"]] = 1
                else:
                    perm_w[lane, col] = 1
    return jnp.asarray(perm_a, BF16), jnp.asarray(perm_w, BF16)


def _select_matrices(heads_per_group, head_dim, q):
    rr = heads_per_group
    base = _piece_base(rr)
    gw = rr * head_dim
    sel_c = np.zeros((GROUP_LANES, 2 * rr * q), np.float32)
    sel_e = np.zeros((GROUP_LANES, 3 * gw), np.float32)
    sel_w = np.zeros((GROUP_LANES, gw), np.float32)
    for piece in range(N_PIECES):
        for r in range(rr):
            k = piece * rr + r
            sel_c[base["cum_f"] + k, r * q:(r + 1) * q] = 1
            sel_c[base["cum_b"] + k, (rr + r) * q:(rr + r + 1) * q] = 1
            sel_e[base["ecum_f"] + k, r * head_dim:(r + 1) * head_dim] = 1
            sel_e[base["ecum_b"] + k, gw + r * head_dim:gw + (r + 1) * head_dim] = 1
            sel_e[base["wend_f"] + k, 2 * gw + r * head_dim:2 * gw + (r + 1) * head_dim] = 1
            sel_w[k, r * head_dim:(r + 1) * head_dim] = 1
    return jnp.asarray(sel_c, BF16), jnp.asarray(sel_e, BF16), jnp.asarray(sel_w, BF16)


def _bf16_pieces(v):
    hi = v.astype(BF16)
    rem = v - hi.astype(F32)
    mid = rem.astype(BF16)
    lo = (rem - mid.astype(F32)).astype(BF16)
    return [hi, mid, lo]


def _dt_kernel(raw_ref, bias_ref, alog_ref, perm_a_ref, perm_w_ref,
               a_ref, w_ref, rowt_ref, diagt_ref, elast_ref, *, n_heads):
    q = raw_ref.shape[0]
    v = raw_ref[...] + bias_ref[...]
    dt = jnp.maximum(v, 0.0) + jnp.log1p(jnp.exp(-jnp.abs(v)))
    a2 = dt * (-jnp.exp(alog_ref[...])) * LOG2_E
    r = lax.broadcasted_iota(jnp.int32, (q, q), 0)
    c = lax.broadcasted_iota(jnp.int32, (q, q), 1)
    hi = lax.Precision.HIGHEST
    cum_f = jnp.dot((c <= r).astype(F32), a2, preferred_element_type=F32, precision=hi)
    cum_b = jnp.dot((c >= r).astype(F32), a2, preferred_element_type=F32, precision=hi)
    fwd = lax.broadcasted_iota(jnp.int32, a2.shape, 1) < n_heads
    cum = jnp.where(fwd, cum_f, cum_b)
    last = jnp.where(fwd[0:1, :], cum[q - 1:q, :], cum[0:1, :])
    ecum = jnp.exp2(cum)
    wend = jnp.exp2(last - cum) * dt
    lhs_a = jnp.concatenate(_bf16_pieces(cum) + _bf16_pieces(ecum) + _bf16_pieces(wend), axis=1)
    a_ref[...] = _dot(lhs_a, perm_a_ref[...]).astype(BF16)
    w_ref[...] = _dot(jnp.concatenate(_bf16_pieces(wend), axis=1), perm_w_ref[...]).astype(BF16)
    rowt_ref[0] = (cum - jnp.log2(dt)).T
    diagt_ref[0] = jnp.log2(dt + pltpu.roll(dt, n_heads, 1)).T
    elast_ref[0] = jnp.exp2(last)


def _dt_prep(dt_raw, dt_bias, a_log, bsz, seq, heads_per_group):
    t, w = dt_raw.shape
    q = CHUNK
    nc = seq // q
    perm_a, perm_w = _perm_matrices(w // 2, heads_per_group)
    gl = N_GROUPS * GROUP_LANES
    col_spec = pl.BlockSpec((q, w), lambda i: (i, 0))
    vec_spec = pl.BlockSpec((1, w), lambda i: (0, 0))
    grp_spec = pl.BlockSpec((q, gl), lambda i: (i, 0))
    return pl.pallas_call(
        functools.partial(_dt_kernel, n_heads=w // 2),
        grid=(t // q,),
        in_specs=[col_spec, vec_spec, vec_spec,
                  pl.BlockSpec(perm_a.shape, lambda i: (0, 0)),
                  pl.BlockSpec(perm_w.shape, lambda i: (0, 0))],
        out_specs=[grp_spec, grp_spec,
                   pl.BlockSpec((1, w, q), lambda i: (i // nc, 0, i % nc)),
                   pl.BlockSpec((1, w, q), lambda i: (i // nc, 0, i % nc)),
                   pl.BlockSpec((1, 1, w), lambda i: (i, 0, 0))],
        out_shape=[jax.ShapeDtypeStruct((t, gl), BF16), jax.ShapeDtypeStruct((t, gl), BF16),
                   jax.ShapeDtypeStruct((bsz, w, seq), F32), jax.ShapeDtypeStruct((bsz, w, seq), F32),
                   jax.ShapeDtypeStruct((t // q, 1, w), F32)],
        compiler_params=_params("arbitrary"),
        name="m2_dt_prep",
    )(dt_raw, dt_bias, a_log, perm_a, perm_w)


def _ssd_backward_block(blk, x_ref, b_ref, w_ref, selw_ref, el_ref, stb, sb_scr, *, q, per_step):
    ks = list(reversed(range(per_step)))
    rows = [slice(k * q, (k + 1) * q) for k in range(per_step)]
    wexp = {k: _dot(w_ref[rows[k], :], selw_ref[...]) for k in ks}
    upd = {}
    for k in ks:
        xw = (x_ref[rows[k], :].astype(F32) * wexp[k]).astype(BF16)
        bt = b_ref[rows[k], :].astype(F32).T.astype(BF16)
        upd[k] = _dot(bt, xw)
    st = stb[...]
    for k in ks:
        sb_scr[blk * per_step + k] = st.astype(BF16)
        st = st * el_ref[0, k, 1:2, :] + upd[k]
    stb[...] = st


def _ssd_forward_block(blk, x_ref, b_ref, c_ref, z_ref, a_ref, selc_ref, sele_ref, rtf_ref, rtb_ref, dg_ref,
                       el_ref, dsk_ref, ng_ref, u_ref, stf, sb_scr, *, q, per_step, heads_per_group, head_dim):
    rr = heads_per_group
    gw = rr * head_dim
    ks = list(range(per_step))
    rows = [slice(k * q, (k + 1) * q) for k in ks]
    t_idx = lax.broadcasted_iota(jnp.int32, (q, q), 0)
    s_idx = lax.broadcasted_iota(jnp.int32, (q, q), 1)
    below = t_idx > s_idx
    above = t_idx < s_idx
    lane_head = lax.broadcasted_iota(jnp.int32, (q, 256), 1) // head_dim
    n_pairs = rr // 2
    heads_per_tile = 256 // head_dim

    scores, col_f, col_b, exp_f, exp_b, exp_w = {}, {}, {}, {}, {}, {}
    for k in ks:
        a = a_ref[rows[k], :]
        scores[k] = lax.dot_general(c_ref[rows[k], :], b_ref[rows[k], :], (((1,), (1,)), ((), ())),
                                    preferred_element_type=F32)
        for pr in range(n_pairs):
            col_f[k, pr] = _dot(a, selc_ref[:, 2 * pr * q:(2 * pr + 2) * q])
            col_b[k, pr] = _dot(a, selc_ref[:, (rr + 2 * pr) * q:(rr + 2 * pr + 2) * q])
        exp_f[k] = _dot(a, sele_ref[:, 0:gw])
        exp_b[k] = _dot(a, sele_ref[:, gw:2 * gw])
        exp_w[k] = _dot(a, sele_ref[:, 2 * gw:3 * gw])

    y, upd = {}, {}
    for k in ks:
        x = x_ref[rows[k], :]
        rtf, rtb, dg = rtf_ref[0, :, rows[k]], rtb_ref[0, :, rows[k]], dg_ref[0, :, rows[k]]
        y_parts = []
        for tile in range(gw // 256):
            xt = x[:, tile * 256:(tile + 1) * 256]
            acc = None
            for pair in range(heads_per_tile // 2):
                pr = tile * (heads_per_tile // 2) + pair
                ms, xs = [], []
                for j in range(2):
                    h = 2 * pr + j
                    cols = slice(j * q, (j + 1) * q)
                    seg = jnp.where(below, col_f[k, pr][:, cols] - rtf[h:h + 1, :],
                                    jnp.where(above, col_b[k, pr][:, cols] - rtb[h:h + 1, :], dg[h:h + 1, :]))
                    ms.append((scores[k] * jnp.exp2(seg)).astype(BF16))
                    xs.append(jnp.where(lane_head == 2 * pair + j, xt, jnp.zeros_like(xt)))
                part = _dot(jnp.concatenate(ms, axis=1), jnp.concatenate(xs, axis=0))
                acc = part if acc is None else acc + part
            y_parts.append(acc)
        xf32 = x.astype(F32)
        y[k] = (jnp.concatenate(y_parts, axis=1) + dsk_ref[...] * xf32
                + exp_b[k] * _dot(c_ref[rows[k], :], sb_scr[blk * per_step + k]))
        xw = (xf32 * exp_w[k]).astype(BF16)
        bt = b_ref[rows[k], :].astype(F32).T.astype(BF16)
        upd[k] = _dot(bt, xw)

    st = stf[...]
    sts = {}
    for k in ks:
        sts[k] = st.astype(BF16)
        st = st * el_ref[0, k, 0:1, :] + upd[k]
    stf[...] = st
    for k in ks:
        yk = y[k] + exp_f[k] * _dot(c_ref[rows[k], :], sts[k])
        u = yk * _silu(z_ref[rows[k], :].astype(F32))
        u_ref[rows[k], :] = _rms(u, ng_ref[...]).astype(BF16)


def _ssd_kernel(x_ref, b_ref, c_ref, z_ref, a_ref, w_ref, selc_ref, sele_ref, selw_ref,
                rtf_ref, rtb_ref, dg_ref, el_ref, dsk_ref, ng_ref,
                u_ref, stf, stb, sb_scr, *, heads_per_group, head_dim, per_step):
    q = x_ref.shape[0] // per_step
    phase = pl.program_id(2)
    step = pl.program_id(3)
    nblk = pl.num_programs(3)

    @pl.when(phase == 0)
    def _():
        @pl.when(step == 0)
        def _():
            stb[...] = jnp.zeros_like(stb)

        _ssd_backward_block(nblk - 1 - step, x_ref, b_ref, w_ref, selw_ref, el_ref, stb, sb_scr,
                            q=q, per_step=per_step)

    @pl.when(phase == 1)
    def _():
        @pl.when(step == 0)
        def _():
            stf[...] = jnp.zeros_like(stf)

        _ssd_forward_block(step, x_ref, b_ref, c_ref, z_ref, a_ref, selc_ref, sele_ref, rtf_ref, rtb_ref, dg_ref,
                           el_ref, dsk_ref, ng_ref, u_ref, stf, sb_scr, q=q, per_step=per_step,
                           heads_per_group=heads_per_group, head_dim=head_dim)


def _ssd(xbc, zxbc, a_cols, w_cols, row_t, diag_t, elast, dskip, norm_g, bsz, seq, d_inner, d_state, n_heads):
    t = xbc.shape[0]
    q = CHUNK
    nc = seq // q
    per_step = SSD_CHUNKS_PER_STEP
    assert nc % per_step == 0
    nblk = nc // per_step
    rows = per_step * q
    g = N_GROUPS
    rr = n_heads // g
    p = d_inner // n_heads
    gw = rr * p
    assert p == 64 and d_state == 128 and q == 128 and gw % 256 == 0 and rr % 2 == 0
    assert 5 * N_PIECES * rr <= GROUP_LANES and d_inner // g == gw
    boff = d_inner // d_state
    sel_c, sel_e, sel_w = _select_matrices(rr, p, q)
    elast = elast.reshape(bsz * nblk, per_step, 2, d_inner)

    def both(b, gi, ph, s):
        return b * nblk + ph * s + (1 - ph) * (nblk - 1 - s)

    def fwd(b, gi, ph, s):
        return b * nblk + ph * s

    def bwd(b, gi, ph, s):
        return b * nblk + (1 - ph) * (nblk - 1 - s)

    const = lambda arr: pl.BlockSpec(arr.shape, lambda b, gi, ph, s: (0, 0))
    return pl.pallas_call(
        functools.partial(_ssd_kernel, heads_per_group=rr, head_dim=p, per_step=per_step),
        grid=(bsz, g, 2, nblk),
        in_specs=[pl.BlockSpec((rows, gw), lambda b, gi, ph, s: (both(b, gi, ph, s), gi)),
                  pl.BlockSpec((rows, d_state), lambda b, gi, ph, s: (both(b, gi, ph, s), boff + gi)),
                  pl.BlockSpec((rows, d_state), lambda b, gi, ph, s: (fwd(b, gi, ph, s), boff + g + gi)),
                  pl.BlockSpec((rows, gw), lambda b, gi, ph, s: (fwd(b, gi, ph, s), gi)),
                  pl.BlockSpec((rows, GROUP_LANES), lambda b, gi, ph, s: (fwd(b, gi, ph, s), gi)),
                  pl.BlockSpec((rows, GROUP_LANES), lambda b, gi, ph, s: (bwd(b, gi, ph, s), gi)),
                  const(sel_c), const(sel_e), const(sel_w),
                  pl.BlockSpec((1, rr, rows), lambda b, gi, ph, s: (b, gi, ph * s)),
                  pl.BlockSpec((1, rr, rows), lambda b, gi, ph, s: (b, g + gi, ph * s)),
                  pl.BlockSpec((1, rr, rows), lambda b, gi, ph, s: (b, gi, ph * s)),
                  pl.BlockSpec((1, per_step, 2, gw), lambda b, gi, ph, s: (both(b, gi, ph, s), 0, 0, gi)),
                  pl.BlockSpec((1, gw), lambda b, gi, ph, s: (0, gi)),
                  pl.BlockSpec((1, gw), lambda b, gi, ph, s: (0, gi))],
        out_specs=pl.BlockSpec((rows, gw), lambda b, gi, ph, s: (fwd(b, gi, ph, s), gi)),
        out_shape=jax.ShapeDtypeStruct((t, d_inner), BF16),
        scratch_shapes=[pltpu.VMEM((d_state, gw), F32), pltpu.VMEM((d_state, gw), F32),
                        pltpu.VMEM((nc, d_state, gw), BF16)],
        compiler_params=_params("arbitrary", "arbitrary", "arbitrary", "arbitrary"),
        name="m2_ssd",
    )(xbc, xbc, xbc, zxbc, a_cols, w_cols, sel_c, sel_e, sel_w, row_t, row_t, diag_t, elast, dskip, norm_g)


def _m2_out_kernel(u_ref, w_ref, x_ref, gate_ref, o_ref):
    o_ref[...] = x_ref[...] + gate_ref[0] * _dot(u_ref[...], w_ref[...])


def _m2_out(u, w_out, x2, gate, seq, tm, tn):
    t, d = x2.shape
    di = u.shape[1]
    tps = seq // tm
    return pl.pallas_call(
        _m2_out_kernel,
        grid=(t // tm, d // tn),
        in_specs=[pl.BlockSpec((tm, di), lambda i, j: (i, 0)),
                  pl.BlockSpec((di, tn), lambda i, j: (0, j)),
                  pl.BlockSpec((tm, tn), lambda i, j: (i, j)),
                  pl.BlockSpec((1, 1, tn), lambda i, j: (i // tps, 0, j))],
        out_specs=pl.BlockSpec((tm, tn), lambda i, j: (i, j)),
        out_shape=jax.ShapeDtypeStruct((t, d), F32),
        compiler_params=_params("arbitrary", "arbitrary"),
        name="m2_out_proj",
    )(u, w_out, x2, gate)


def _row_tile(seq, want):
    tm = min(want, seq)
    assert seq % tm == 0 and tm % HALO_ROWS == 0
    return tm


def kernel(x, c, ada_w, ada_b, norm_g, final_g, sc_in_w, sc_conv_w, sc_out_w, m2_in_w, m2_conv_w, m2_conv_b,
           m2_dt_bias, m2_a_log, m2_d, m2_norm_g, m2_out_w, mlp_w1, mlp_w2):
    bsz, seq, d = x.shape
    t = bsz * seq
    d_inner = m2_norm_g.shape[1]
    n_heads = m2_d.shape[1]
    conv_dim = m2_conv_w.shape[1]
    d_state = (conv_dim - d_inner) // (2 * N_GROUPS)
    head_dim = d_inner // n_heads
    assert seq % CHUNK == 0
    tm = _row_tile(seq, 512)

    mods = _modulation(c, ada_w, ada_b)
    x2 = x.reshape(t, d)
    vec = lambda v: v.reshape(1, -1)

    shift, scale, gate = mods[0]
    bg, cv = _sc_in(x2, vec(norm_g[0, 0]), scale, shift, sc_in_w[0].astype(BF16), seq, tm, 512)
    x2 = _sc_out(cv, bg, sc_conv_w[0].T, sc_out_w[0].astype(BF16), x2, gate, seq, tm)
    shift, scale, gate = mods[1]
    x2 = _mlp(x2, vec(norm_g[0, 1]), scale, shift, gate, mlp_w1[0].astype(BF16), mlp_w2[0].astype(BF16),
              vec(final_g), seq, tm, 512, final_norm=False)

    shift, scale, gate = mods[2]
    n_main = d_inner + conv_dim
    w_in = m2_in_w[0].astype(BF16)
    zxbc, dt_raw = _m2_in(x2, vec(norm_g[1, 0]), scale, shift, w_in[:, :n_main], w_in[:, n_main:], seq, tm, 1024)
    xbc = _conv_silu(zxbc, m2_conv_w[0].T, vec(m2_conv_b[0]), d_inner, seq, tm, 1024)
    rr = n_heads // N_GROUPS
    a_cols, w_cols, row_t, diag_t, elast = _dt_prep(dt_raw, vec(m2_dt_bias[0]), vec(m2_a_log[0]), bsz, seq, rr)
    elast = jnp.repeat(elast.reshape(-1, 2, n_heads), head_dim, axis=2)
    dskip = jnp.repeat(m2_d[0], head_dim).reshape(1, d_inner)
    u = _ssd(xbc, zxbc, a_cols, w_cols, row_t, diag_t, elast, dskip, vec(m2_norm_g[0]),
             bsz, seq, d_inner, d_state, n_heads)
    x2 = _m2_out(u, m2_out_w[0].astype(BF16), x2, gate, seq, tm, 512)
    shift, scale, gate = mods[3]
    x2 = _mlp(x2, vec(norm_g[1, 1]), scale, shift, gate, mlp_w1[1].astype(BF16), mlp_w2[1].astype(BF16),
              vec(final_g), seq, tm, 512, final_norm=True)
    return x2.reshape(bsz, seq, d)
```

```python
import functools

import numpy as np
import jax
import jax.numpy as jnp
from jax import lax
from jax.experimental import pallas as pl
from jax.experimental.pallas import tpu as pltpu

N_GROUPS = 8
CHUNK = 128
SSD_CHUNKS_PER_STEP = 8
EPS = 1e-6
LOG2_E = 1.4426950408889634
SUBLANES = 8
LANES = 128
HALO_ROWS = 16
VMEM_LIMIT_BYTES = 56 * 1024 * 1024

F32 = jnp.float32
BF16 = jnp.bfloat16


def _params(*sem):
    return pltpu.CompilerParams(dimension_semantics=sem, vmem_limit_bytes=VMEM_LIMIT_BYTES)


def _dot(a, b):
    return jnp.dot(a, b, preferred_element_type=F32)


def _silu(x):
    half = 0.5 * x
    return half * jnp.tanh(half) + half


def _rms(x, g):
    ms = jnp.mean(x * x, axis=-1, keepdims=True)
    return (x * lax.rsqrt(ms + EPS)) * g


def _modnorm(x, g, scale, shift):
    return _rms(x, g) * (1.0 + scale) + shift


def _conv3(main, prev_row, next_row, w):
    tm, width = main.shape
    core = w[0:1, :] * pltpu.roll(main, 1, 0) + w[1:2, :] * main + w[2:3, :] * pltpu.roll(main, tm - 1, 0)
    row = lax.broadcasted_iota(jnp.int32, (SUBLANES, width), 0)
    first = jnp.where(row == 0, w[0:1, :] * (prev_row - main[tm - 1:tm, :]), 0.0)
    last = jnp.where(row == SUBLANES - 1, w[2:3, :] * (next_row - main[0:1, :]), 0.0)
    return jnp.concatenate([core[:SUBLANES] + first, core[SUBLANES:tm - SUBLANES], core[tm - SUBLANES:] + last],
                           axis=0)


def _halo_rows(prev_ref, next_ref, tiles_per_seq):
    i = pl.program_id(0) % tiles_per_seq
    prev = prev_ref[...].astype(F32)[HALO_ROWS - 1:HALO_ROWS, :]
    nxt = next_ref[...].astype(F32)[0:1, :]
    prev = jnp.where(i == 0, 0.0, prev)
    nxt = jnp.where(i == tiles_per_seq - 1, 0.0, nxt)
    return prev, nxt


def _with_casts(body, n_in, n_out, n_cast):
    def kern(*refs):
        ins, rest = refs[:n_in], refs[n_in:]
        cast_in, rest = rest[:n_cast], rest[n_cast:]
        outs, rest = rest[:n_out], rest[n_out:]
        cast_out, scratch = rest[:n_cast], rest[n_cast:]
        for src, dst in zip(cast_in, cast_out):
            dst[...] = src[...].astype(BF16)
        body(*ins, *outs, *scratch)
    return kern


def _cast_specs(w, n_steps, lin):
    r, c = w.shape
    nb = n_steps
    while r % nb or (r // nb) % HALO_ROWS:
        nb //= 2
    imap = lambda *idx: (lin(*idx) * nb // n_steps, 0)
    spec = pl.BlockSpec((r // nb, c), imap)
    return spec, spec, jax.ShapeDtypeStruct((r, c), BF16)


def _cast_args(weights, n_steps, lin):
    specs = [_cast_specs(w, n_steps, lin) for w in weights]
    return [s[0] for s in specs], [s[1] for s in specs], [s[2] for s in specs]


def _mod_kernel(c_ref, w_ref, b_ref, o_ref, cond_scr):
    @pl.when((pl.program_id(0) == 0) & (pl.program_id(1) == 0))
    def _():
        cond_scr[...] = _silu(c_ref[...])

    for slab in range(w_ref.shape[2] // LANES):
        cols = slice(slab * LANES, (slab + 1) * LANES)
        w = w_ref[0, :, cols]
        for b in range(c_ref.shape[0]):
            o_ref[0, b:b + 1, cols] = jnp.sum(w * cond_scr[b], axis=0, keepdims=True) + b_ref[0, :, cols]


def _modulation(c, ada_w, ada_b):
    bsz, d = c.shape
    n_mod = ada_w.shape[0] * ada_w.shape[1]
    w = ada_w.reshape(n_mod, d, 3 * d)
    b = ada_b.reshape(n_mod, 1, 3 * d)
    c_rep = jnp.broadcast_to(c[:, :, None], (bsz, d, LANES))
    tn = 768
    out = pl.pallas_call(
        _mod_kernel,
        grid=(n_mod, 3 * d // tn),
        in_specs=[pl.BlockSpec((bsz, d, LANES), lambda m, j: (0, 0, 0)),
                  pl.BlockSpec((1, d, tn), lambda m, j: (m, 0, j)),
                  pl.BlockSpec((1, 1, tn), lambda m, j: (m, 0, j))],
        out_specs=pl.BlockSpec((1, bsz, tn), lambda m, j: (m, 0, j)),
        out_shape=jax.ShapeDtypeStruct((n_mod, bsz, 3 * d), F32),
        scratch_shapes=[pltpu.VMEM((bsz, d, LANES), F32)],
        compiler_params=_params("arbitrary", "arbitrary"),
        name="modulation",
    )(c_rep, w, b)
    out = out.reshape(n_mod, bsz, 3, 1, d)
    return [(out[m, :, 0], out[m, :, 1], out[m, :, 2]) for m in range(n_mod)]


def _sc_in_kernel(x_ref, g_ref, sc_ref, sh_ref, wb_ref, wc_ref, wv_ref, b_out, cv_out, h_scr):
    def project(h):
        b_out[...] = _dot(h, wb_ref[...]).astype(BF16)
        cv_out[...] = (_dot(h, wc_ref[...]) * _dot(h, wv_ref[...])).astype(BF16)

    @pl.when(pl.program_id(1) == 0)
    def _():
        h = _modnorm(x_ref[...], g_ref[...], sc_ref[0], sh_ref[0]).astype(BF16)
        h_scr[...] = h
        project(h)

    @pl.when(pl.program_id(1) != 0)
    def _():
        project(h_scr[...])


def _sc_in(x2, g, scale, shift, w_in, seq, tm, tn, to_cast):
    t, d = x2.shape
    tps = seq // tm
    nb = d // tn
    grid = (t // tm, nb)
    cast_in, cast_out, cast_shape = _cast_args(to_cast, grid[0] * grid[1], lambda i, j: i * nb + j)
    row_spec = pl.BlockSpec((tm, d), lambda i, j: (i, 0))
    vec_spec = pl.BlockSpec((1, d), lambda i, j: (0, 0))
    mod_spec = pl.BlockSpec((1, 1, d), lambda i, j: (i // tps, 0, 0))
    out_spec = pl.BlockSpec((tm, tn), lambda i, j: (i, j))
    outs = pl.pallas_call(
        _with_casts(_sc_in_kernel, 7, 2, len(to_cast)),
        grid=grid,
        in_specs=[row_spec, vec_spec, mod_spec, mod_spec,
                  pl.BlockSpec((d, tn), lambda i, j: (0, j)),
                  pl.BlockSpec((d, tn), lambda i, j: (0, j + nb)),
                  pl.BlockSpec((d, tn), lambda i, j: (0, j + 2 * nb))] + cast_in,
        out_specs=[out_spec, out_spec] + cast_out,
        out_shape=[jax.ShapeDtypeStruct((t, d), BF16), jax.ShapeDtypeStruct((t, d), BF16)] + cast_shape,
        scratch_shapes=[pltpu.VMEM((tm, d), BF16)],
        compiler_params=_params("arbitrary", "arbitrary"),
        name="sc_in_proj",
    )(x2, g, scale, shift, w_in, w_in, w_in, *to_cast)
    return outs[0], outs[1], outs[2:]


def _sc_out_kernel(cv_ref, cvp_ref, cvn_ref, bg_ref, cw_ref, w_ref, x_ref, gate_ref, o_ref, *, tiles_per_seq):
    prev, nxt = _halo_rows(cvp_ref, cvn_ref, tiles_per_seq)
    u = _conv3(cv_ref[...].astype(F32), prev, nxt, cw_ref[...])
    u = (u * bg_ref[...].astype(F32)).astype(BF16)
    o_ref[...] = x_ref[...] + gate_ref[0] * _dot(u, w_ref[...])


def _sc_out(cv, bg, conv_w, w_out, x2, gate, seq, tm):
    t, d = x2.shape
    tps = seq // tm
    hb = tm // HALO_ROWS
    last = t // HALO_ROWS - 1
    row_spec = pl.BlockSpec((tm, d), lambda i: (i, 0))
    return pl.pallas_call(
        functools.partial(_sc_out_kernel, tiles_per_seq=tps),
        grid=(t // tm,),
        in_specs=[row_spec,
                  pl.BlockSpec((HALO_ROWS, d), lambda i: (jnp.maximum(i * hb - 1, 0), 0)),
                  pl.BlockSpec((HALO_ROWS, d), lambda i: (jnp.minimum((i + 1) * hb, last), 0)),
                  row_spec,
                  pl.BlockSpec((3, d), lambda i: (0, 0)),
                  pl.BlockSpec((d, d), lambda i: (0, 0)),
                  row_spec,
                  pl.BlockSpec((1, 1, d), lambda i: (i // tps, 0, 0))],
        out_specs=row_spec,
        out_shape=jax.ShapeDtypeStruct((t, d), F32),
        compiler_params=_params("arbitrary"),
        name="sc_conv_out_proj",
    )(cv, cv, cv, bg, conv_w, w_out, x2, gate)


def _mlp_kernel(x_ref, g_ref, sc_ref, sh_ref, gate_ref, w1_ref, w2_ref, fg_ref, o_ref, h_scr, *, final_norm):
    k = pl.program_id(1)

    def hidden_chunk(h):
        a = jnp.maximum(_dot(h, w1_ref[...]), 0.0)
        return _dot((a * a).astype(BF16), w2_ref[...])

    @pl.when(k == 0)
    def _():
        h = _modnorm(x_ref[...], g_ref[...], sc_ref[0], sh_ref[0]).astype(BF16)
        h_scr[...] = h
        o_ref[...] = hidden_chunk(h)

    @pl.when(k != 0)
    def _():
        o_ref[...] += hidden_chunk(h_scr[...])

    @pl.when(k == pl.num_programs(1) - 1)
    def _():
        y = x_ref[...] + gate_ref[0] * o_ref[...]
        if final_norm:
            y = _rms(y, fg_ref[...])
        o_ref[...] = y


def _mlp(x2, g, scale, shift, gate, w1, w2, final_g, seq, tm, tf, final_norm, to_cast=()):
    t, d = x2.shape
    ff = w1.shape[1]
    tps = seq // tm
    nk = ff // tf
    grid = (t // tm, nk)
    cast_in, cast_out, cast_shape = _cast_args(to_cast, grid[0] * grid[1], lambda i, k: i * nk + k)
    row_spec = pl.BlockSpec((tm, d), lambda i, k: (i, 0))
    vec_spec = pl.BlockSpec((1, d), lambda i, k: (0, 0))
    mod_spec = pl.BlockSpec((1, 1, d), lambda i, k: (i // tps, 0, 0))
    outs = pl.pallas_call(
        _with_casts(functools.partial(_mlp_kernel, final_norm=final_norm), 8, 1, len(to_cast)),
        grid=grid,
        in_specs=[row_spec, vec_spec, mod_spec, mod_spec, mod_spec,
                  pl.BlockSpec((d, tf), lambda i, k: (0, k)),
                  pl.BlockSpec((tf, d), lambda i, k: (k, 0)),
                  vec_spec] + cast_in,
        out_specs=[row_spec] + cast_out,
        out_shape=[jax.ShapeDtypeStruct((t, d), F32)] + cast_shape,
        scratch_shapes=[pltpu.VMEM((tm, d), BF16)],
        compiler_params=_params("arbitrary", "arbitrary"),
        name="mlp_final" if final_norm else "mlp",
    )(x2, g, scale, shift, gate, w1, w2, final_g, *to_cast)
    return outs[0], outs[1:]


def _m2_in_kernel(x_ref, g_ref, sc_ref, sh_ref, w_ref, wdt_ref, o_ref, dt_ref, h_scr):
    @pl.when(pl.program_id(1) == 0)
    def _():
        h = _modnorm(x_ref[...], g_ref[...], sc_ref[0], sh_ref[0]).astype(BF16)
        h_scr[...] = h
        dt_ref[...] = _dot(h, wdt_ref[...])
        o_ref[...] = _dot(h, w_ref[...]).astype(BF16)

    @pl.when(pl.program_id(1) != 0)
    def _():
        o_ref[...] = _dot(h_scr[...], w_ref[...]).astype(BF16)


def _m2_in(x2, g, scale, shift, w_in, n, seq, tm, tn):
    t, d = x2.shape
    ndt = w_in.shape[1] - n
    assert n % tn == 0 and n % ndt == 0
    tps = seq // tm
    return pl.pallas_call(
        _m2_in_kernel,
        grid=(t // tm, n // tn),
        in_specs=[pl.BlockSpec((tm, d), lambda i, j: (i, 0)),
                  pl.BlockSpec((1, d), lambda i, j: (0, 0)),
                  pl.BlockSpec((1, 1, d), lambda i, j: (i // tps, 0, 0)),
                  pl.BlockSpec((1, 1, d), lambda i, j: (i // tps, 0, 0)),
                  pl.BlockSpec((d, tn), lambda i, j: (0, j)),
                  pl.BlockSpec((d, ndt), lambda i, j: (0, n // ndt))],
        out_specs=[pl.BlockSpec((tm, tn), lambda i, j: (i, j)),
                   pl.BlockSpec((tm, ndt), lambda i, j: (i, 0))],
        out_shape=[jax.ShapeDtypeStruct((t, n), BF16), jax.ShapeDtypeStruct((t, ndt), F32)],
        scratch_shapes=[pltpu.VMEM((tm, d), BF16)],
        compiler_params=_params("arbitrary", "arbitrary"),
        name="m2_in_proj",
    )(x2, g, scale, shift, w_in, w_in)


def _conv_silu_kernel(m_ref, p_ref, n_ref, w_ref, b_ref, o_ref, *, tiles_per_seq):
    prev, nxt = _halo_rows(p_ref, n_ref, tiles_per_seq)
    u = _conv3(m_ref[...].astype(F32), prev, nxt, w_ref[...]) + b_ref[...]
    o_ref[...] = _silu(u).astype(BF16)


def _conv_silu(zxbc, conv_w, conv_b, d_inner, seq, tm, tn):
    t = zxbc.shape[0]
    cdim = conv_w.shape[1]
    tps = seq // tm
    off = d_inner // tn
    hb = tm // HALO_ROWS
    last = t // HALO_ROWS - 1
    return pl.pallas_call(
        functools.partial(_conv_silu_kernel, tiles_per_seq=tps),
        grid=(t // tm, cdim // tn),
        in_specs=[pl.BlockSpec((tm, tn), lambda i, j: (i, j + off)),
                  pl.BlockSpec((HALO_ROWS, tn), lambda i, j: (jnp.maximum(i * hb - 1, 0), j + off)),
                  pl.BlockSpec((HALO_ROWS, tn), lambda i, j: (jnp.minimum((i + 1) * hb, last), j + off)),
                  pl.BlockSpec((3, tn), lambda i, j: (0, j)),
                  pl.BlockSpec((1, tn), lambda i, j: (0, j))],
        out_specs=pl.BlockSpec((tm, tn), lambda i, j: (i, j)),
        out_shape=jax.ShapeDtypeStruct((t, cdim), BF16),
        compiler_params=_params("arbitrary", "arbitrary"),
        name="m2_conv_silu",
    )(zxbc, zxbc, zxbc, conv_w, conv_b)


N_PIECES = 3
GROUP_LANES = 128


def _piece_base(heads_per_group):
    w = N_PIECES * heads_per_group
    return {"cum_f": 0, "cum_b": w, "ecum_f": 2 * w, "ecum_b": 3 * w, "wend_f": 4 * w}


def _perm_matrices(n_heads, heads_per_group):
    rr = heads_per_group
    base = _piece_base(rr)
    nat = 2 * n_heads
    perm_a = np.zeros((3 * N_PIECES * nat, N_GROUPS * GROUP_LANES), np.float32)
    perm_w = np.zeros((N_PIECES * nat, N_GROUPS * GROUP_LANES), np.float32)
    for piece in range(N_PIECES):
        for d in range(2):
            for h in range(n_heads):
                g, r = divmod(h, rr)
                lane = piece * nat + d * n_heads + h
                col = g * GROUP_LANES + piece * rr + r
                perm_a[0 * N_PIECES * nat + lane, col + (base["cum_f"] if d == 0 else base["cum_b"])] = 1
                perm_a[1 * N_PIECES * nat + lane, col + (base["ecum_f"] if d == 0 else base["ecum_b"])] = 1
                if d == 0:
                    perm_a[2 * N_PIECES * nat + lane, col + base["wend_f"]] = 1
                else:
                    perm_w[lane, col] = 1
    return jnp.asarray(perm_a, BF16), jnp.asarray(perm_w, BF16)


def _select_matrices(heads_per_group, head_dim, q):
    rr = heads_per_group
    base = _piece_base(rr)
    gw = rr * head_dim
    sel_c = np.zeros((GROUP_LANES, 2 * rr * q), np.float32)
    sel_e = np.zeros((GROUP_LANES, 3 * gw), np.float32)
    sel_w = np.zeros((GROUP_LANES, gw), np.float32)
    for piece in range(N_PIECES):
        for r in range(rr):
            k = piece * rr + r
            sel_c[base["cum_f"] + k, r * q:(r + 1) * q] = 1
            sel_c[base["cum_b"] + k, (rr + r) * q:(rr + r + 1) * q] = 1
            sel_e[base["ecum_f"] + k, r * head_dim:(r + 1) * head_dim] = 1
            sel_e[base["ecum_b"] + k, gw + r * head_dim:gw + (r + 1) * head_dim] = 1
            sel_e[base["wend_f"] + k, 2 * gw + r * head_dim:2 * gw + (r + 1) * head_dim] = 1
            sel_w[k, r * head_dim:(r + 1) * head_dim] = 1
    return jnp.asarray(sel_c, BF16), jnp.asarray(sel_e, BF16), jnp.asarray(sel_w, BF16)


def _bf16_pieces(v):
    hi = v.astype(BF16)
    rem = v - hi.astype(F32)
    mid = rem.astype(BF16)
    lo = (rem - mid.astype(F32)).astype(BF16)
    return [hi, mid, lo]


def _dt_kernel(raw_ref, bias_ref, alog_ref, perm_a_ref, perm_w_ref,
               a_ref, w_ref, rowt_ref, diagt_ref, elast_ref, *, n_heads):
    q = raw_ref.shape[0]
    v = raw_ref[...] + bias_ref[...]
    dt = jnp.maximum(v, 0.0) + jnp.log1p(jnp.exp(-jnp.abs(v)))
    a2 = dt * (-jnp.exp(alog_ref[...])) * LOG2_E
    r = lax.broadcasted_iota(jnp.int32, (q, q), 0)
    c = lax.broadcasted_iota(jnp.int32, (q, q), 1)
    hi = lax.Precision.HIGHEST
    cum_f = jnp.dot((c <= r).astype(F32), a2, preferred_element_type=F32, precision=hi)
    cum_b = jnp.dot((c >= r).astype(F32), a2, preferred_element_type=F32, precision=hi)
    fwd = lax.broadcasted_iota(jnp.int32, a2.shape, 1) < n_heads
    cum = jnp.where(fwd, cum_f, cum_b)
    last = jnp.where(fwd[0:1, :], cum[q - 1:q, :], cum[0:1, :])
    ecum = jnp.exp2(cum)
    wend = jnp.exp2(last - cum) * dt
    lhs_a = jnp.concatenate(_bf16_pieces(cum) + _bf16_pieces(ecum) + _bf16_pieces(wend), axis=1)
    a_ref[...] = _dot(lhs_a, perm_a_ref[...]).astype(BF16)
    w_ref[...] = _dot(jnp.concatenate(_bf16_pieces(wend), axis=1), perm_w_ref[...]).astype(BF16)
    rowt_ref[0] = (cum - jnp.log2(dt)).T
    diagt_ref[0] = jnp.log2(dt + pltpu.roll(dt, n_heads, 1)).T
    elast_ref[0] = jnp.exp2(last)


def _dt_prep(dt_raw, dt_bias, a_log, bsz, seq, heads_per_group):
    t, w = dt_raw.shape
    q = CHUNK
    nc = seq // q
    perm_a, perm_w = _perm_matrices(w // 2, heads_per_group)
    gl = N_GROUPS * GROUP_LANES
    col_spec = pl.BlockSpec((q, w), lambda i: (i, 0))
    vec_spec = pl.BlockSpec((1, w), lambda i: (0, 0))
    grp_spec = pl.BlockSpec((q, gl), lambda i: (i, 0))
    return pl.pallas_call(
        functools.partial(_dt_kernel, n_heads=w // 2),
        grid=(t // q,),
        in_specs=[col_spec, vec_spec, vec_spec,
                  pl.BlockSpec(perm_a.shape, lambda i: (0, 0)),
                  pl.BlockSpec(perm_w.shape, lambda i: (0, 0))],
        out_specs=[grp_spec, grp_spec,
                   pl.BlockSpec((1, w, q), lambda i: (i // nc, 0, i % nc)),
                   pl.BlockSpec((1, w, q), lambda i: (i // nc, 0, i % nc)),
                   pl.BlockSpec((1, 1, w), lambda i: (i, 0, 0))],
        out_shape=[jax.ShapeDtypeStruct((t, gl), BF16), jax.ShapeDtypeStruct((t, gl), BF16),
                   jax.ShapeDtypeStruct((bsz, w, seq), F32), jax.ShapeDtypeStruct((bsz, w, seq), F32),
                   jax.ShapeDtypeStruct((t // q, 1, w), F32)],
        compiler_params=_params("arbitrary"),
        name="m2_dt_prep",
    )(dt_raw, dt_bias, a_log, perm_a, perm_w)


def _ssd_backward_block(blk, x_ref, b_ref, w_ref, selw_ref, el_ref, stb, sb_scr, *, q, per_step):
    ks = list(reversed(range(per_step)))
    rows = [slice(k * q, (k + 1) * q) for k in range(per_step)]
    wexp = {k: _dot(w_ref[rows[k], :], selw_ref[...]) for k in ks}
    upd = {}
    for k in ks:
        xw = (x_ref[rows[k], :].astype(F32) * wexp[k]).astype(BF16)
        bt = b_ref[rows[k], :].astype(F32).T.astype(BF16)
        upd[k] = _dot(bt, xw)
    st = stb[...]
    for k in ks:
        sb_scr[blk * per_step + k] = st.astype(BF16)
        st = st * el_ref[0, k, 1:2, :] + upd[k]
    stb[...] = st


def _ssd_forward_block(blk, x_ref, b_ref, c_ref, z_ref, a_ref, selc_ref, sele_ref, rtf_ref, rtb_ref, dg_ref,
                       el_ref, dsk_ref, ng_ref, u_ref, stf, sb_scr, *, q, per_step, heads_per_group, head_dim):
    rr = heads_per_group
    gw = rr * head_dim
    ks = list(range(per_step))
    rows = [slice(k * q, (k + 1) * q) for k in ks]
    t_idx = lax.broadcasted_iota(jnp.int32, (q, q), 0)
    s_idx = lax.broadcasted_iota(jnp.int32, (q, q), 1)
    below = t_idx > s_idx
    above = t_idx < s_idx
    lane_head = lax.broadcasted_iota(jnp.int32, (q, 256), 1) // head_dim
    n_pairs = rr // 2
    heads_per_tile = 256 // head_dim

    scores, col_f, col_b, exp_f, exp_b, exp_w = {}, {}, {}, {}, {}, {}
    for k in ks:
        a = a_ref[rows[k], :]
        scores[k] = lax.dot_general(c_ref[rows[k], :], b_ref[rows[k], :], (((1,), (1,)), ((), ())),
                                    preferred_element_type=F32)
        for pr in range(n_pairs):
            col_f[k, pr] = _dot(a, selc_ref[:, 2 * pr * q:(2 * pr + 2) * q])
            col_b[k, pr] = _dot(a, selc_ref[:, (rr + 2 * pr) * q:(rr + 2 * pr + 2) * q])
        exp_f[k] = _dot(a, sele_ref[:, 0:gw])
        exp_b[k] = _dot(a, sele_ref[:, gw:2 * gw])
        exp_w[k] = _dot(a, sele_ref[:, 2 * gw:3 * gw])

    y, upd = {}, {}
    for k in ks:
        x = x_ref[rows[k], :]
        rtf, rtb, dg = rtf_ref[0, :, rows[k]], rtb_ref[0, :, rows[k]], dg_ref[0, :, rows[k]]
        y_parts = []
        for tile in range(gw // 256):
            xt = x[:, tile * 256:(tile + 1) * 256]
            acc = None
            for pair in range(heads_per_tile // 2):
                pr = tile * (heads_per_tile // 2) + pair
                ms, xs = [], []
                for j in range(2):
                    h = 2 * pr + j
                    cols = slice(j * q, (j + 1) * q)
                    seg = jnp.where(below, col_f[k, pr][:, cols] - rtf[h:h + 1, :],
                                    jnp.where(above, col_b[k, pr][:, cols] - rtb[h:h + 1, :], dg[h:h + 1, :]))
                    ms.append((scores[k] * jnp.exp2(seg)).astype(BF16))
                    xs.append(jnp.where(lane_head == 2 * pair + j, xt, jnp.zeros_like(xt)))
                part = _dot(jnp.concatenate(ms, axis=1), jnp.concatenate(xs, axis=0))
                acc = part if acc is None else acc + part
            y_parts.append(acc)
        xf32 = x.astype(F32)
        y[k] = (jnp.concatenate(y_parts, axis=1) + dsk_ref[...] * xf32
                + exp_b[k] * _dot(c_ref[rows[k], :], sb_scr[blk * per_step + k]))
        xw = (xf32 * exp_w[k]).astype(BF16)
        bt = b_ref[rows[k], :].astype(F32).T.astype(BF16)
        upd[k] = _dot(bt, xw)

    st = stf[...]
    sts = {}
    for k in ks:
        sts[k] = st.astype(BF16)
        st = st * el_ref[0, k, 0:1, :] + upd[k]
    stf[...] = st
    for k in ks:
        yk = y[k] + exp_f[k] * _dot(c_ref[rows[k], :], sts[k])
        u = yk * _silu(z_ref[rows[k], :].astype(F32))
        u_ref[rows[k], :] = _rms(u, ng_ref[...]).astype(BF16)


def _ssd_kernel(x_ref, b_ref, c_ref, z_ref, a_ref, w_ref, selc_ref, sele_ref, selw_ref,
                rtf_ref, rtb_ref, dg_ref, el_ref, dsk_ref, ng_ref,
                u_ref, stf, stb, sb_scr, *, heads_per_group, head_dim, per_step):
    q = x_ref.shape[0] // per_step
    phase = pl.program_id(2)
    step = pl.program_id(3)
    nblk = pl.num_programs(3)

    @pl.when(phase == 0)
    def _():
        @pl.when(step == 0)
        def _():
            stb[...] = jnp.zeros_like(stb)

        _ssd_backward_block(nblk - 1 - step, x_ref, b_ref, w_ref, selw_ref, el_ref, stb, sb_scr,
                            q=q, per_step=per_step)

    @pl.when(phase == 1)
    def _():
        @pl.when(step == 0)
        def _():
            stf[...] = jnp.zeros_like(stf)

        _ssd_forward_block(step, x_ref, b_ref, c_ref, z_ref, a_ref, selc_ref, sele_ref, rtf_ref, rtb_ref, dg_ref,
                           el_ref, dsk_ref, ng_ref, u_ref, stf, sb_scr, q=q, per_step=per_step,
                           heads_per_group=heads_per_group, head_dim=head_dim)


def _ssd(xbc, zxbc, a_cols, w_cols, row_t, diag_t, elast, dskip, norm_g, bsz, seq, d_inner, d_state, n_heads,
         to_cast):
    t = xbc.shape[0]
    q = CHUNK
    nc = seq // q
    per_step = SSD_CHUNKS_PER_STEP
    assert nc % per_step == 0
    nblk = nc // per_step
    rows = per_step * q
    g = N_GROUPS
    rr = n_heads // g
    p = d_inner // n_heads
    gw = rr * p
    assert p == 64 and d_state == 128 and q == 128 and gw % 256 == 0 and rr % 2 == 0
    assert 5 * N_PIECES * rr <= GROUP_LANES and d_inner // g == gw
    boff = d_inner // d_state
    sel_c, sel_e, sel_w = _select_matrices(rr, p, q)
    elast = elast.reshape(bsz * nblk, per_step, 2, d_inner)

    def both(b, gi, ph, s):
        return b * nblk + ph * s + (1 - ph) * (nblk - 1 - s)

    def fwd(b, gi, ph, s):
        return b * nblk + ph * s

    def bwd(b, gi, ph, s):
        return b * nblk + (1 - ph) * (nblk - 1 - s)

    const = lambda arr: pl.BlockSpec(arr.shape, lambda b, gi, ph, s: (0, 0))
    grid = (bsz, g, 2, nblk)
    cast_in, cast_out, cast_shape = _cast_args(to_cast, bsz * g * 2 * nblk,
                                               lambda b, gi, ph, s: ((b * g + gi) * 2 + ph) * nblk + s)
    outs = pl.pallas_call(
        _with_casts(functools.partial(_ssd_kernel, heads_per_group=rr, head_dim=p, per_step=per_step),
                    15, 1, len(to_cast)),
        grid=grid,
        in_specs=[pl.BlockSpec((rows, gw), lambda b, gi, ph, s: (both(b, gi, ph, s), gi)),
                  pl.BlockSpec((rows, d_state), lambda b, gi, ph, s: (both(b, gi, ph, s), boff + gi)),
                  pl.BlockSpec((rows, d_state), lambda b, gi, ph, s: (fwd(b, gi, ph, s), boff + g + gi)),
                  pl.BlockSpec((rows, gw), lambda b, gi, ph, s: (fwd(b, gi, ph, s), gi)),
                  pl.BlockSpec((rows, GROUP_LANES), lambda b, gi, ph, s: (fwd(b, gi, ph, s), gi)),
                  pl.BlockSpec((rows, GROUP_LANES), lambda b, gi, ph, s: (bwd(b, gi, ph, s), gi)),
                  const(sel_c), const(sel_e), const(sel_w),
                  pl.BlockSpec((1, rr, rows), lambda b, gi, ph, s: (b, gi, ph * s)),
                  pl.BlockSpec((1, rr, rows), lambda b, gi, ph, s: (b, g + gi, ph * s)),
                  pl.BlockSpec((1, rr, rows), lambda b, gi, ph, s: (b, gi, ph * s)),
                  pl.BlockSpec((1, per_step, 2, gw), lambda b, gi, ph, s: (both(b, gi, ph, s), 0, 0, gi)),
                  pl.BlockSpec((1, gw), lambda b, gi, ph, s: (0, gi)),
                  pl.BlockSpec((1, gw), lambda b, gi, ph, s: (0, gi))] + cast_in,
        out_specs=[pl.BlockSpec((rows, gw), lambda b, gi, ph, s: (fwd(b, gi, ph, s), gi))] + cast_out,
        out_shape=[jax.ShapeDtypeStruct((t, d_inner), BF16)] + cast_shape,
        scratch_shapes=[pltpu.VMEM((d_state, gw), F32), pltpu.VMEM((d_state, gw), F32),
                        pltpu.VMEM((nc, d_state, gw), BF16)],
        compiler_params=_params("arbitrary", "arbitrary", "arbitrary", "arbitrary"),
        name="m2_ssd",
    )(xbc, xbc, xbc, zxbc, a_cols, w_cols, sel_c, sel_e, sel_w, row_t, row_t, diag_t, elast, dskip, norm_g,
      *to_cast)
    return outs[0], outs[1:]


def _m2_out_kernel(u_ref, w_ref, x_ref, gate_ref, o_ref):
    o_ref[...] = x_ref[...] + gate_ref[0] * _dot(u_ref[...], w_ref[...])


def _m2_out(u, w_out, x2, gate, seq, tm, tn):
    t, d = x2.shape
    di = u.shape[1]
    tps = seq // tm
    return pl.pallas_call(
        _m2_out_kernel,
        grid=(t // tm, d // tn),
        in_specs=[pl.BlockSpec((tm, di), lambda i, j: (i, 0)),
                  pl.BlockSpec((di, tn), lambda i, j: (0, j)),
                  pl.BlockSpec((tm, tn), lambda i, j: (i, j)),
                  pl.BlockSpec((1, 1, tn), lambda i, j: (i // tps, 0, j))],
        out_specs=pl.BlockSpec((tm, tn), lambda i, j: (i, j)),
        out_shape=jax.ShapeDtypeStruct((t, d), F32),
        compiler_params=_params("arbitrary", "arbitrary"),
        name="m2_out_proj",
    )(u, w_out, x2, gate)


def _row_tile(seq, want):
    tm = min(want, seq)
    assert seq % tm == 0 and tm % HALO_ROWS == 0
    return tm


def _tiles(seq):
    return dict(tm=_row_tile(seq, 1024), tm_conv=_row_tile(seq, 512), tn=512, tn_wide=1024, tf=512)


def kernel(x, c, ada_w, ada_b, norm_g, final_g, sc_in_w, sc_conv_w, sc_out_w, m2_in_w, m2_conv_w, m2_conv_b,
           m2_dt_bias, m2_a_log, m2_d, m2_norm_g, m2_out_w, mlp_w1, mlp_w2):
    bsz, seq, d = x.shape
    t = bsz * seq
    d_inner = m2_norm_g.shape[1]
    n_heads = m2_d.shape[1]
    conv_dim = m2_conv_w.shape[1]
    d_state = (conv_dim - d_inner) // (2 * N_GROUPS)
    head_dim = d_inner // n_heads
    assert seq % CHUNK == 0
    ts = _tiles(seq)
    tm, tn, tf = ts["tm"], ts["tn"], ts["tf"]

    mods = _modulation(c, ada_w, ada_b)
    x2 = x.reshape(t, d)
    vec = lambda v: v.reshape(1, -1)

    shift, scale, gate = mods[0]
    bg, cv, (sc_out_bf, w1_bf, w2_bf) = _sc_in(x2, vec(norm_g[0, 0]), scale, shift, sc_in_w[0].astype(BF16),
                                               seq, tm, tn, [sc_out_w[0], mlp_w1[0], mlp_w2[0]])
    x2 = _sc_out(cv, bg, sc_conv_w[0].T, sc_out_bf, x2, gate, seq, ts["tm_conv"])
    shift, scale, gate = mods[1]
    x2, (m2_in_bf,) = _mlp(x2, vec(norm_g[0, 1]), scale, shift, gate, w1_bf, w2_bf,
                           vec(final_g), seq, tm, tf, final_norm=False, to_cast=[m2_in_w[0]])

    shift, scale, gate = mods[2]
    zxbc, dt_raw = _m2_in(x2, vec(norm_g[1, 0]), scale, shift, m2_in_bf, d_inner + conv_dim,
                          seq, tm, ts["tn_wide"])
    xbc = _conv_silu(zxbc, m2_conv_w[0].T, vec(m2_conv_b[0]), d_inner, seq, ts["tm_conv"], ts["tn_wide"])
    rr = n_heads // N_GROUPS
    a_cols, w_cols, row_t, diag_t, elast = _dt_prep(dt_raw, vec(m2_dt_bias[0]), vec(m2_a_log[0]), bsz, seq, rr)
    elast = jnp.repeat(elast.reshape(-1, 2, n_heads), head_dim, axis=2)
    dskip = jnp.repeat(m2_d[0], head_dim).reshape(1, d_inner)
    u, (m2_out_bf, w1_bf, w2_bf) = _ssd(xbc, zxbc, a_cols, w_cols, row_t, diag_t, elast, dskip, vec(m2_norm_g[0]),
                                        bsz, seq, d_inner, d_state, n_heads,
                                        [m2_out_w[0], mlp_w1[1], mlp_w2[1]])
    x2 = _m2_out(u, m2_out_bf, x2, gate, seq, tm, tn)
    shift, scale, gate = mods[3]
    x2, _ = _mlp(x2, vec(norm_g[1, 1]), scale, shift, gate, w1_bf, w2_bf,
                 vec(final_g), seq, tm, tf, final_norm=True)
    return x2.reshape(bsz, seq, d)
```

```python
import functools

import numpy as np
import jax
import jax.numpy as jnp
from jax import lax
from jax.experimental import pallas as pl
from jax.experimental.pallas import tpu as pltpu

N_GROUPS = 8
CHUNK = 128
DT_CHUNKS_PER_STEP = 4
SSD_CHUNKS_PER_STEP = 8
EPS = 1e-6
LOG2_E = 1.4426950408889634
SUBLANES = 8
LANES = 128
HALO_ROWS = 16
VMEM_LIMIT_BYTES = 56 * 1024 * 1024

F32 = jnp.float32
BF16 = jnp.bfloat16


def _params(*sem):
    return pltpu.CompilerParams(dimension_semantics=sem, vmem_limit_bytes=VMEM_LIMIT_BYTES)


def _dot(a, b):
    return jnp.dot(a, b, preferred_element_type=F32)


def _silu_of_half(half):
    return half * jnp.tanh(half) + half


def _silu(x):
    return _silu_of_half(0.5 * x)


def _rms(x, g):
    ms = jnp.mean(x * x, axis=-1, keepdims=True)
    return (x * lax.rsqrt(ms + EPS)) * g


def _modnorm(x, g, scale, shift):
    return _rms(x, g) * (1.0 + scale) + shift


def _conv3(main, prev_row, next_row, w):
    tm, width = main.shape
    core = w[0:1, :] * pltpu.roll(main, 1, 0) + w[1:2, :] * main + w[2:3, :] * pltpu.roll(main, tm - 1, 0)
    row = lax.broadcasted_iota(jnp.int32, (SUBLANES, width), 0)
    first = jnp.where(row == 0, w[0:1, :] * (prev_row - main[tm - 1:tm, :]), 0.0)
    last = jnp.where(row == SUBLANES - 1, w[2:3, :] * (next_row - main[0:1, :]), 0.0)
    return jnp.concatenate([core[:SUBLANES] + first, core[SUBLANES:tm - SUBLANES], core[tm - SUBLANES:] + last],
                           axis=0)


def _halo_rows(prev_ref, next_ref, tiles_per_seq):
    i = pl.program_id(0) % tiles_per_seq
    prev = prev_ref[...].astype(F32)[HALO_ROWS - 1:HALO_ROWS, :]
    nxt = next_ref[...].astype(F32)[0:1, :]
    prev = jnp.where(i == 0, 0.0, prev)
    nxt = jnp.where(i == tiles_per_seq - 1, 0.0, nxt)
    return prev, nxt


def _with_casts(body, n_in, n_out, n_cast):
    def kern(*refs):
        ins, rest = refs[:n_in], refs[n_in:]
        cast_in, rest = rest[:n_cast], rest[n_cast:]
        outs, rest = rest[:n_out], rest[n_out:]
        cast_out, scratch = rest[:n_cast], rest[n_cast:]
        for src, dst in zip(cast_in, cast_out):
            dst[...] = src[...].astype(BF16)
        body(*ins, *outs, *scratch)
    return kern


def _cast_specs(stacked, layer, n_steps, lin):
    _, r, c = stacked.shape
    nb = n_steps
    while r % nb or (r // nb) % HALO_ROWS:
        nb //= 2
    src = pl.BlockSpec((None, r // nb, c), lambda *idx: (layer, lin(*idx) * nb // n_steps, 0))
    dst = pl.BlockSpec((r // nb, c), lambda *idx: (lin(*idx) * nb // n_steps, 0))
    return src, dst, jax.ShapeDtypeStruct((r, c), BF16)


def _cast_args(to_cast, n_steps, lin):
    specs = [_cast_specs(w, layer, n_steps, lin) for w, layer in to_cast]
    return [s[0] for s in specs], [s[1] for s in specs], [s[2] for s in specs]


def _mod_kernel(c_ref, w_ref, b_ref, o_ref, cond_scr):
    @pl.when((pl.program_id(0) == 0) & (pl.program_id(1) == 0))
    def _():
        cond_scr[...] = _silu(c_ref[...])

    for slab in range(w_ref.shape[2] // LANES):
        cols = slice(slab * LANES, (slab + 1) * LANES)
        w = w_ref[0, :, cols]
        for b in range(c_ref.shape[0]):
            o_ref[0, b:b + 1, cols] = jnp.sum(w * cond_scr[b], axis=0, keepdims=True) + b_ref[0, :, cols]


def _modulation(c, ada_w, ada_b):
    bsz, d = c.shape
    n_mod = ada_w.shape[0] * ada_w.shape[1]
    w = ada_w.reshape(n_mod, d, 3 * d)
    b = ada_b.reshape(n_mod, 1, 3 * d)
    c_rep = jnp.broadcast_to(c[:, :, None], (bsz, d, LANES))
    tn = 768
    out = pl.pallas_call(
        _mod_kernel,
        grid=(n_mod, 3 * d // tn),
        in_specs=[pl.BlockSpec((bsz, d, LANES), lambda m, j: (0, 0, 0)),
                  pl.BlockSpec((1, d, tn), lambda m, j: (m, 0, j)),
                  pl.BlockSpec((1, 1, tn), lambda m, j: (m, 0, j))],
        out_specs=pl.BlockSpec((1, bsz, tn), lambda m, j: (m, 0, j)),
        out_shape=jax.ShapeDtypeStruct((n_mod, bsz, 3 * d), F32),
        scratch_shapes=[pltpu.VMEM((bsz, d, LANES), F32)],
        compiler_params=_params("arbitrary", "arbitrary"),
        name="modulation",
    )(c_rep, w, b)
    out = out.reshape(n_mod, bsz, 3, 1, d)
    return [(out[m, :, 0], out[m, :, 1], out[m, :, 2]) for m in range(n_mod)]


def _sc_in_kernel(x_ref, g_ref, sc_ref, sh_ref, wb_ref, wc_ref, wv_ref, b_out, cv_out, h_scr):
    def project(h):
        b_out[...] = _dot(h, wb_ref[...]).astype(BF16)
        cv_out[...] = (_dot(h, wc_ref[...]) * _dot(h, wv_ref[...])).astype(BF16)

    @pl.when(pl.program_id(1) == 0)
    def _():
        h = _modnorm(x_ref[...], g_ref[...], sc_ref[0], sh_ref[0]).astype(BF16)
        h_scr[...] = h
        project(h)

    @pl.when(pl.program_id(1) != 0)
    def _():
        project(h_scr[...])


def _sc_in(x2, g, scale, shift, w_in, seq, tm, tn, to_cast):
    t, d = x2.shape
    tps = seq // tm
    nb = d // tn
    grid = (t // tm, nb)
    cast_in, cast_out, cast_shape = _cast_args(to_cast, grid[0] * grid[1], lambda i, j: i * nb + j)
    row_spec = pl.BlockSpec((tm, d), lambda i, j: (i, 0))
    vec_spec = pl.BlockSpec((1, d), lambda i, j: (0, 0))
    mod_spec = pl.BlockSpec((1, 1, d), lambda i, j: (i // tps, 0, 0))
    out_spec = pl.BlockSpec((tm, tn), lambda i, j: (i, j))
    outs = pl.pallas_call(
        _with_casts(_sc_in_kernel, 7, 2, len(to_cast)),
        grid=grid,
        in_specs=[row_spec, vec_spec, mod_spec, mod_spec,
                  pl.BlockSpec((d, tn), lambda i, j: (0, j)),
                  pl.BlockSpec((d, tn), lambda i, j: (0, j + nb)),
                  pl.BlockSpec((d, tn), lambda i, j: (0, j + 2 * nb))] + cast_in,
        out_specs=[out_spec, out_spec] + cast_out,
        out_shape=[jax.ShapeDtypeStruct((t, d), BF16), jax.ShapeDtypeStruct((t, d), BF16)] + cast_shape,
        scratch_shapes=[pltpu.VMEM((tm, d), BF16)],
        compiler_params=_params("arbitrary", "arbitrary"),
        name="sc_in_proj",
    )(x2, g, scale, shift, w_in, w_in, w_in, *[w for w, _ in to_cast])
    return outs[0], outs[1], outs[2:]


def _sc_out_kernel(cv_ref, cvp_ref, cvn_ref, bg_ref, cw_ref, w_ref, x_ref, gate_ref, o_ref, *, tiles_per_seq):
    prev, nxt = _halo_rows(cvp_ref, cvn_ref, tiles_per_seq)
    u = _conv3(cv_ref[...].astype(F32), prev, nxt, cw_ref[...])
    u = (u * bg_ref[...].astype(F32)).astype(BF16)
    o_ref[...] = x_ref[...] + gate_ref[0] * _dot(u, w_ref[...])


def _sc_out(cv, bg, conv_w, w_out, x2, gate, seq, tm):
    t, d = x2.shape
    tps = seq // tm
    hb = tm // HALO_ROWS
    last = t // HALO_ROWS - 1
    row_spec = pl.BlockSpec((tm, d), lambda i: (i, 0))
    return pl.pallas_call(
        functools.partial(_sc_out_kernel, tiles_per_seq=tps),
        grid=(t // tm,),
        in_specs=[row_spec,
                  pl.BlockSpec((HALO_ROWS, d), lambda i: (jnp.maximum(i * hb - 1, 0), 0)),
                  pl.BlockSpec((HALO_ROWS, d), lambda i: (jnp.minimum((i + 1) * hb, last), 0)),
                  row_spec,
                  pl.BlockSpec((3, d), lambda i: (0, 0)),
                  pl.BlockSpec((d, d), lambda i: (0, 0)),
                  row_spec,
                  pl.BlockSpec((1, 1, d), lambda i: (i // tps, 0, 0))],
        out_specs=row_spec,
        out_shape=jax.ShapeDtypeStruct((t, d), F32),
        compiler_params=_params("arbitrary"),
        name="sc_conv_out_proj",
    )(cv, cv, cv, bg, conv_w, w_out, x2, gate)


def _mlp_kernel(x_ref, g_ref, sc_ref, sh_ref, gate_ref, w1_ref, w2_ref, fg_ref, o_ref, h_scr, *, final_norm):
    k = pl.program_id(1)

    def hidden_chunk(h):
        a = jnp.maximum(_dot(h, w1_ref[...]), 0.0)
        return _dot((a * a).astype(BF16), w2_ref[...])

    @pl.when(k == 0)
    def _():
        h = _modnorm(x_ref[...], g_ref[...], sc_ref[0], sh_ref[0]).astype(BF16)
        h_scr[...] = h
        o_ref[...] = hidden_chunk(h)

    last = pl.num_programs(1) - 1

    @pl.when((k != 0) & (k != last))
    def _():
        o_ref[...] += hidden_chunk(h_scr[...])

    @pl.when(k == last)
    def _():
        y = x_ref[...] + gate_ref[0] * (o_ref[...] + hidden_chunk(h_scr[...]))
        if final_norm:
            y = _rms(y, fg_ref[...])
        o_ref[...] = y


def _mlp(x2, g, scale, shift, gate, w1, w2, final_g, seq, tm, tf, final_norm, to_cast=()):
    t, d = x2.shape
    ff = w1.shape[1]
    tps = seq // tm
    nk = ff // tf
    assert nk >= 2
    grid = (t // tm, nk)
    cast_in, cast_out, cast_shape = _cast_args(to_cast, grid[0] * grid[1], lambda i, k: i * nk + k)
    row_spec = pl.BlockSpec((tm, d), lambda i, k: (i, 0))
    vec_spec = pl.BlockSpec((1, d), lambda i, k: (0, 0))
    mod_spec = pl.BlockSpec((1, 1, d), lambda i, k: (i // tps, 0, 0))
    outs = pl.pallas_call(
        _with_casts(functools.partial(_mlp_kernel, final_norm=final_norm), 8, 1, len(to_cast)),
        grid=grid,
        in_specs=[row_spec, vec_spec, mod_spec, mod_spec, mod_spec,
                  pl.BlockSpec((d, tf), lambda i, k: (0, k)),
                  pl.BlockSpec((tf, d), lambda i, k: (k, 0)),
                  vec_spec] + cast_in,
        out_specs=[row_spec] + cast_out,
        out_shape=[jax.ShapeDtypeStruct((t, d), F32)] + cast_shape,
        scratch_shapes=[pltpu.VMEM((tm, d), BF16)],
        compiler_params=_params("arbitrary", "arbitrary"),
        name="mlp_final" if final_norm else "mlp",
    )(x2, g, scale, shift, gate, w1, w2, final_g, *[w for w, _ in to_cast])
    return outs[0], outs[1:]


def _m2_in_kernel(x_ref, g_ref, sc_ref, sh_ref, w_ref, wdt_ref, o_ref, dt_ref, h_scr):
    @pl.when(pl.program_id(1) == 0)
    def _():
        h = _modnorm(x_ref[...], g_ref[...], sc_ref[0], sh_ref[0]).astype(BF16)
        h_scr[...] = h
        dt_ref[...] = _dot(h, wdt_ref[...])
        o_ref[...] = _dot(h, w_ref[...]).astype(BF16)

    @pl.when(pl.program_id(1) != 0)
    def _():
        o_ref[...] = _dot(h_scr[...], w_ref[...]).astype(BF16)


def _m2_in(x2, g, scale, shift, w_in, n, seq, tm, tn):
    t, d = x2.shape
    ndt = w_in.shape[1] - n
    assert n % tn == 0 and n % ndt == 0
    tps = seq // tm
    return pl.pallas_call(
        _m2_in_kernel,
        grid=(t // tm, n // tn),
        in_specs=[pl.BlockSpec((tm, d), lambda i, j: (i, 0)),
                  pl.BlockSpec((1, d), lambda i, j: (0, 0)),
                  pl.BlockSpec((1, 1, d), lambda i, j: (i // tps, 0, 0)),
                  pl.BlockSpec((1, 1, d), lambda i, j: (i // tps, 0, 0)),
                  pl.BlockSpec((d, tn), lambda i, j: (0, j)),
                  pl.BlockSpec((d, ndt), lambda i, j: (0, n // ndt))],
        out_specs=[pl.BlockSpec((tm, tn), lambda i, j: (i, j)),
                   pl.BlockSpec((tm, ndt), lambda i, j: (i, 0))],
        out_shape=[jax.ShapeDtypeStruct((t, n), BF16), jax.ShapeDtypeStruct((t, ndt), F32)],
        scratch_shapes=[pltpu.VMEM((tm, d), BF16)],
        compiler_params=_params("arbitrary", "arbitrary"),
        name="m2_in_proj",
    )(x2, g, scale, shift, w_in, w_in)


CONV_SLAB = 128


def _shift_matrix():
    s, h = CONV_SLAB, HALO_ROWS
    m = np.zeros((2 * s, s + 2 * h), np.float32)
    for r in range(s):
        m[r, r - 1 if r > 0 else s + h - 1] = 1
        m[s + r, r + 1 if r < s - 1 else s + h] = 1
    return jnp.asarray(m, BF16)


def _conv_silu_kernel(m_ref, p_ref, n_ref, w_ref, b_ref, sh_ref, o_ref, *, tiles_per_seq):
    i = pl.program_id(0) % tiles_per_seq
    s, h = CONV_SLAB, HALO_ROWS
    n_slabs = m_ref.shape[0] // s
    zero = jnp.zeros(p_ref.shape, BF16)
    before_tile = jnp.where(i == 0, zero, p_ref[...])
    after_tile = jnp.where(i == tiles_per_seq - 1, zero, n_ref[...])
    w = 0.5 * w_ref[...]
    bias = 0.5 * b_ref[...]
    for k in range(n_slabs):
        slab = m_ref[k * s:(k + 1) * s, :]
        before = before_tile if k == 0 else m_ref[k * s - h:k * s, :]
        after = after_tile if k == n_slabs - 1 else m_ref[(k + 1) * s:(k + 1) * s + h, :]
        shifted = _dot(sh_ref[...], jnp.concatenate([slab, before, after], axis=0))
        half = w[0:1, :] * shifted[:s] + w[1:2, :] * slab.astype(F32) + w[2:3, :] * shifted[s:] + bias
        o_ref[k * s:(k + 1) * s, :] = _silu_of_half(half).astype(BF16)


def _conv_silu(zxbc, conv_w, conv_b, d_inner, seq, tm, tn):
    t = zxbc.shape[0]
    cdim = conv_w.shape[1]
    tps = seq // tm
    off = d_inner // tn
    hb = tm // HALO_ROWS
    last = t // HALO_ROWS - 1
    shift = _shift_matrix()
    assert tm % CONV_SLAB == 0
    return pl.pallas_call(
        functools.partial(_conv_silu_kernel, tiles_per_seq=tps),
        grid=(t // tm, cdim // tn),
        in_specs=[pl.BlockSpec((tm, tn), lambda i, j: (i, j + off)),
                  pl.BlockSpec((HALO_ROWS, tn), lambda i, j: (jnp.maximum(i * hb - 1, 0), j + off)),
                  pl.BlockSpec((HALO_ROWS, tn), lambda i, j: (jnp.minimum((i + 1) * hb, last), j + off)),
                  pl.BlockSpec((3, tn), lambda i, j: (0, j)),
                  pl.BlockSpec((1, tn), lambda i, j: (0, j)),
                  pl.BlockSpec(shift.shape, lambda i, j: (0, 0))],
        out_specs=pl.BlockSpec((tm, tn), lambda i, j: (i, j)),
        out_shape=jax.ShapeDtypeStruct((t, cdim), BF16),
        compiler_params=_params("arbitrary", "arbitrary"),
        name="m2_conv_silu",
    )(zxbc, zxbc, zxbc, conv_w, conv_b, shift)


N_PIECES = 3
GROUP_LANES = 128


def _piece_base(heads_per_group):
    w = N_PIECES * heads_per_group
    return {"cum_f": 0, "cum_b": w, "ecum_f": 2 * w, "ecum_b": 3 * w, "wend_f": 4 * w}


def _perm_matrices(n_heads, heads_per_group):
    rr = heads_per_group
    base = _piece_base(rr)
    nat = 2 * n_heads
    perm_a = np.zeros((3 * N_PIECES * nat, N_GROUPS * GROUP_LANES), np.float32)
    perm_w = np.zeros((N_PIECES * nat, N_GROUPS * GROUP_LANES), np.float32)
    for piece in range(N_PIECES):
        for d in range(2):
            for h in range(n_heads):
                g, r = divmod(h, rr)
                lane = piece * nat + d * n_heads + h
                col = g * GROUP_LANES + piece * rr + r
                perm_a[0 * N_PIECES * nat + lane, col + (base["cum_f"] if d == 0 else base["cum_b"])] = 1
                perm_a[1 * N_PIECES * nat + lane, col + (base["ecum_f"] if d == 0 else base["ecum_b"])] = 1
                if d == 0:
                    perm_a[2 * N_PIECES * nat + lane, col + base["wend_f"]] = 1
                else:
                    perm_w[lane, col] = 1
    return jnp.asarray(perm_a, BF16), jnp.asarray(perm_w, BF16)


def _select_matrices(heads_per_group, head_dim, q):
    rr = heads_per_group
    base = _piece_base(rr)
    gw = rr * head_dim
    sel_c = np.zeros((GROUP_LANES, 2 * rr * q), np.float32)
    sel_e = np.zeros((GROUP_LANES, 3 * gw), np.float32)
    sel_w = np.zeros((GROUP_LANES, gw), np.float32)
    for piece in range(N_PIECES):
        for r in range(rr):
            k = piece * rr + r
            sel_c[base["cum_f"] + k, r * q:(r + 1) * q] = 1
            sel_c[base["cum_b"] + k, (rr + r) * q:(rr + r + 1) * q] = 1
            sel_e[base["ecum_f"] + k, r * head_dim:(r + 1) * head_dim] = 1
            sel_e[base["ecum_b"] + k, gw + r * head_dim:gw + (r + 1) * head_dim] = 1
            sel_e[base["wend_f"] + k, 2 * gw + r * head_dim:2 * gw + (r + 1) * head_dim] = 1
            sel_w[k, r * head_dim:(r + 1) * head_dim] = 1
    return jnp.asarray(sel_c, BF16), jnp.asarray(sel_e, BF16), jnp.asarray(sel_w, BF16)


def _bf16_pieces(v):
    hi = v.astype(BF16)
    rem = v - hi.astype(F32)
    mid = rem.astype(BF16)
    lo = (rem - mid.astype(F32)).astype(BF16)
    return [hi, mid, lo]


def _dt_kernel(raw_ref, bias_ref, alog_ref, perm_a_ref, perm_w_ref,
               a_ref, w_ref, rowt_ref, diagt_ref, elast_ref, *, n_heads):
    q = CHUNK
    w = raw_ref.shape[1]
    r = lax.broadcasted_iota(jnp.int32, (q, q), 0)
    c = lax.broadcasted_iota(jnp.int32, (q, q), 1)
    lower = jnp.where(c <= r, 1.0, 0.0).astype(BF16)
    upper = jnp.where(c >= r, 1.0, 0.0).astype(BF16)
    fwd = lax.broadcasted_iota(jnp.int32, (q, w), 1) < n_heads
    neg_a2 = jnp.exp(alog_ref[...]) * LOG2_E
    lhs_a, lhs_w = [], []
    for k in range(raw_ref.shape[0] // q):
        rows = slice(k * q, (k + 1) * q)
        v = raw_ref[rows, :] + bias_ref[...]
        dt = jnp.maximum(v, 0.0) + jnp.log1p(jnp.exp(-jnp.abs(v)))
        a2 = -(dt * neg_a2)
        pieces = jnp.concatenate(_bf16_pieces(a2), axis=1)
        pre = _dot(lower, pieces)
        suf = _dot(upper, pieces)
        cum = jnp.where(fwd, pre[:, :w] + pre[:, w:2 * w] + pre[:, 2 * w:],
                        suf[:, :w] + suf[:, w:2 * w] + suf[:, 2 * w:])
        last = jnp.where(fwd[0:1, :], cum[q - 1:q, :], cum[0:1, :])
        wend = _bf16_pieces(jnp.exp2(last - cum) * dt)
        lhs_a.append(jnp.concatenate(_bf16_pieces(cum) + _bf16_pieces(jnp.exp2(cum)) + wend, axis=1))
        lhs_w.append(jnp.concatenate(wend, axis=1))
        rowt_ref[0, :, rows] = (cum - jnp.log2(dt)).T
        diagt_ref[0, :, rows] = jnp.log2(dt + pltpu.roll(dt, n_heads, 1)).T
        elast_ref[k] = jnp.exp2(last)
    a_ref[...] = _dot(jnp.concatenate(lhs_a, axis=0), perm_a_ref[...]).astype(BF16)
    w_ref[...] = _dot(jnp.concatenate(lhs_w, axis=0), perm_w_ref[...]).astype(BF16)


def _dt_prep(dt_raw, dt_bias, a_log, bsz, seq, heads_per_group):
    t, w = dt_raw.shape
    q = CHUNK
    nc = seq // q
    perm_a, perm_w = _perm_matrices(w // 2, heads_per_group)
    gl = N_GROUPS * GROUP_LANES
    per_step = DT_CHUNKS_PER_STEP
    assert nc % per_step == 0
    rows = per_step * q
    nblk = nc // per_step
    col_spec = pl.BlockSpec((rows, w), lambda i: (i, 0))
    vec_spec = pl.BlockSpec((1, w), lambda i: (0, 0))
    grp_spec = pl.BlockSpec((rows, gl), lambda i: (i, 0))
    return pl.pallas_call(
        functools.partial(_dt_kernel, n_heads=w // 2),
        grid=(t // rows,),
        in_specs=[col_spec, vec_spec, vec_spec,
                  pl.BlockSpec(perm_a.shape, lambda i: (0, 0)),
                  pl.BlockSpec(perm_w.shape, lambda i: (0, 0))],
        out_specs=[grp_spec, grp_spec,
                   pl.BlockSpec((1, w, rows), lambda i: (i // nblk, 0, i % nblk)),
                   pl.BlockSpec((1, w, rows), lambda i: (i // nblk, 0, i % nblk)),
                   pl.BlockSpec((per_step, 1, w), lambda i: (i, 0, 0))],
        out_shape=[jax.ShapeDtypeStruct((t, gl), BF16), jax.ShapeDtypeStruct((t, gl), BF16),
                   jax.ShapeDtypeStruct((bsz, w, seq), F32), jax.ShapeDtypeStruct((bsz, w, seq), F32),
                   jax.ShapeDtypeStruct((t // q, 1, w), F32)],
        compiler_params=_params("arbitrary"),
        name="m2_dt_prep",
    )(dt_raw, dt_bias, a_log, perm_a, perm_w)


def _ssd_backward_block(blk, x_ref, b_ref, w_ref, selw_ref, el_ref, stb, sb_scr, *, q, per_step):
    ks = list(reversed(range(per_step)))
    rows = [slice(k * q, (k + 1) * q) for k in range(per_step)]
    wexp = {k: _dot(w_ref[rows[k], :], selw_ref[...]) for k in ks}
    upd = {}
    for k in ks:
        xw = (x_ref[rows[k], :].astype(F32) * wexp[k]).astype(BF16)
        bt = b_ref[rows[k], :].astype(F32).T.astype(BF16)
        upd[k] = _dot(bt, xw)
    st = stb[...]
    for k in ks:
        sb_scr[blk * per_step + k] = st.astype(BF16)
        st = st * el_ref[0, k, 1:2, :] + upd[k]
    stb[...] = st


def _ssd_forward_block(blk, x_ref, b_ref, c_ref, z_ref, a_ref, selc_ref, sele_ref, rtf_ref, rtb_ref, dg_ref,
                       el_ref, dsk_ref, ng_ref, u_ref, stf, sb_scr, *, q, per_step, heads_per_group, head_dim):
    rr = heads_per_group
    gw = rr * head_dim
    ks = list(range(per_step))
    rows = [slice(k * q, (k + 1) * q) for k in ks]
    t_idx = lax.broadcasted_iota(jnp.int32, (q, q), 0)
    s_idx = lax.broadcasted_iota(jnp.int32, (q, q), 1)
    below = t_idx > s_idx
    above = t_idx < s_idx
    lane_head = lax.broadcasted_iota(jnp.int32, (q, 256), 1) // head_dim
    n_pairs = rr // 2
    heads_per_tile = 256 // head_dim

    scores, col_f, col_b, exp_f, exp_b, exp_w = {}, {}, {}, {}, {}, {}
    for k in ks:
        a = a_ref[rows[k], :]
        scores[k] = lax.dot_general(c_ref[rows[k], :], b_ref[rows[k], :], (((1,), (1,)), ((), ())),
                                    preferred_element_type=F32)
        for pr in range(n_pairs):
            col_f[k, pr] = _dot(a, selc_ref[:, 2 * pr * q:(2 * pr + 2) * q])
            col_b[k, pr] = _dot(a, selc_ref[:, (rr + 2 * pr) * q:(rr + 2 * pr + 2) * q])
        exp_f[k] = _dot(a, sele_ref[:, 0:gw])
        exp_b[k] = _dot(a, sele_ref[:, gw:2 * gw])
        exp_w[k] = _dot(a, sele_ref[:, 2 * gw:3 * gw])

    y, upd = {}, {}
    for k in ks:
        x = x_ref[rows[k], :]
        rtf, rtb, dg = rtf_ref[0, :, rows[k]], rtb_ref[0, :, rows[k]], dg_ref[0, :, rows[k]]
        y_parts = []
        for tile in range(gw // 256):
            xt = x[:, tile * 256:(tile + 1) * 256]
            acc = None
            for pair in range(heads_per_tile // 2):
                pr = tile * (heads_per_tile // 2) + pair
                ms, xs = [], []
                for j in range(2):
                    h = 2 * pr + j
                    cols = slice(j * q, (j + 1) * q)
                    seg = jnp.where(below, col_f[k, pr][:, cols] - rtf[h:h + 1, :],
                                    jnp.where(above, col_b[k, pr][:, cols] - rtb[h:h + 1, :], dg[h:h + 1, :]))
                    ms.append((scores[k] * jnp.exp2(seg)).astype(BF16))
                    xs.append(jnp.where(lane_head == 2 * pair + j, xt, jnp.zeros_like(xt)))
                part = _dot(jnp.concatenate(ms, axis=1), jnp.concatenate(xs, axis=0))
                acc = part if acc is None else acc + part
            y_parts.append(acc)
        xf32 = x.astype(F32)
        y[k] = (jnp.concatenate(y_parts, axis=1) + dsk_ref[...] * xf32
                + exp_b[k] * _dot(c_ref[rows[k], :], sb_scr[blk * per_step + k]))
        xw = (xf32 * exp_w[k]).astype(BF16)
        bt = b_ref[rows[k], :].astype(F32).T.astype(BF16)
        upd[k] = _dot(bt, xw)

    st = stf[...]
    sts = {}
    for k in ks:
        sts[k] = st.astype(BF16)
        st = st * el_ref[0, k, 0:1, :] + upd[k]
    stf[...] = st
    for k in ks:
        yk = y[k] + exp_f[k] * _dot(c_ref[rows[k], :], sts[k])
        u = yk * _silu(z_ref[rows[k], :].astype(F32))
        u_ref[rows[k], :] = _rms(u, ng_ref[...]).astype(BF16)


def _ssd_kernel(x_ref, b_ref, c_ref, z_ref, a_ref, w_ref, selc_ref, sele_ref, selw_ref,
                rtf_ref, rtb_ref, dg_ref, el_ref, dsk_ref, ng_ref,
                u_ref, stf, stb, sb_scr, *, heads_per_group, head_dim, per_step):
    q = x_ref.shape[0] // per_step
    phase = pl.program_id(2)
    step = pl.program_id(3)
    nblk = pl.num_programs(3)

    @pl.when(phase == 0)
    def _():
        @pl.when(step == 0)
        def _():
            stb[...] = jnp.zeros_like(stb)

        _ssd_backward_block(nblk - 1 - step, x_ref, b_ref, w_ref, selw_ref, el_ref, stb, sb_scr,
                            q=q, per_step=per_step)

    @pl.when(phase == 1)
    def _():
        @pl.when(step == 0)
        def _():
            stf[...] = jnp.zeros_like(stf)

        _ssd_forward_block(step, x_ref, b_ref, c_ref, z_ref, a_ref, selc_ref, sele_ref, rtf_ref, rtb_ref, dg_ref,
                           el_ref, dsk_ref, ng_ref, u_ref, stf, sb_scr, q=q, per_step=per_step,
                           heads_per_group=heads_per_group, head_dim=head_dim)


def _ssd(xbc, zxbc, a_cols, w_cols, row_t, diag_t, elast, dskip, norm_g, bsz, seq, d_inner, d_state, n_heads,
         to_cast):
    t = xbc.shape[0]
    q = CHUNK
    nc = seq // q
    per_step = SSD_CHUNKS_PER_STEP
    assert nc % per_step == 0
    nblk = nc // per_step
    rows = per_step * q
    g = N_GROUPS
    rr = n_heads // g
    p = d_inner // n_heads
    gw = rr * p
    assert p == 64 and d_state == 128 and q == 128 and gw % 256 == 0 and rr % 2 == 0
    assert 5 * N_PIECES * rr <= GROUP_LANES and d_inner // g == gw
    boff = d_inner // d_state
    sel_c, sel_e, sel_w = _select_matrices(rr, p, q)
    elast = elast.reshape(bsz * nblk, per_step, 2, d_inner)

    def both(b, gi, ph, s):
        return b * nblk + ph * s + (1 - ph) * (nblk - 1 - s)

    def fwd(b, gi, ph, s):
        return b * nblk + ph * s

    def bwd(b, gi, ph, s):
        return b * nblk + (1 - ph) * (nblk - 1 - s)

    const = lambda arr: pl.BlockSpec(arr.shape, lambda b, gi, ph, s: (0, 0))
    grid = (bsz, g, 2, nblk)
    cast_in, cast_out, cast_shape = _cast_args(to_cast, bsz * g * 2 * nblk,
                                               lambda b, gi, ph, s: ((b * g + gi) * 2 + ph) * nblk + s)
    outs = pl.pallas_call(
        _with_casts(functools.partial(_ssd_kernel, heads_per_group=rr, head_dim=p, per_step=per_step),
                    15, 1, len(to_cast)),
        grid=grid,
        in_specs=[pl.BlockSpec((rows, gw), lambda b, gi, ph, s: (both(b, gi, ph, s), gi)),
                  pl.BlockSpec((rows, d_state), lambda b, gi, ph, s: (both(b, gi, ph, s), boff + gi)),
                  pl.BlockSpec((rows, d_state), lambda b, gi, ph, s: (fwd(b, gi, ph, s), boff + g + gi)),
                  pl.BlockSpec((rows, gw), lambda b, gi, ph, s: (fwd(b, gi, ph, s), gi)),
                  pl.BlockSpec((rows, GROUP_LANES), lambda b, gi, ph, s: (fwd(b, gi, ph, s), gi)),
                  pl.BlockSpec((rows, GROUP_LANES), lambda b, gi, ph, s: (bwd(b, gi, ph, s), gi)),
                  const(sel_c), const(sel_e), const(sel_w),
                  pl.BlockSpec((1, rr, rows), lambda b, gi, ph, s: (b, gi, ph * s)),
                  pl.BlockSpec((1, rr, rows), lambda b, gi, ph, s: (b, g + gi, ph * s)),
                  pl.BlockSpec((1, rr, rows), lambda b, gi, ph, s: (b, gi, ph * s)),
                  pl.BlockSpec((1, per_step, 2, gw), lambda b, gi, ph, s: (both(b, gi, ph, s), 0, 0, gi)),
                  pl.BlockSpec((1, gw), lambda b, gi, ph, s: (0, gi)),
                  pl.BlockSpec((1, gw), lambda b, gi, ph, s: (0, gi))] + cast_in,
        out_specs=[pl.BlockSpec((rows, gw), lambda b, gi, ph, s: (fwd(b, gi, ph, s), gi))] + cast_out,
        out_shape=[jax.ShapeDtypeStruct((t, d_inner), BF16)] + cast_shape,
        scratch_shapes=[pltpu.VMEM((d_state, gw), F32), pltpu.VMEM((d_state, gw), F32),
                        pltpu.VMEM((nc, d_state, gw), BF16)],
        compiler_params=_params("arbitrary", "arbitrary", "arbitrary", "arbitrary"),
        name="m2_ssd",
    )(xbc, xbc, xbc, zxbc, a_cols, w_cols, sel_c, sel_e, sel_w, row_t, row_t, diag_t, elast, dskip, norm_g,
      *[w for w, _ in to_cast])
    return outs[0], outs[1:]


def _m2_out_kernel(u_ref, w_ref, x_ref, gate_ref, o_ref):
    o_ref[...] = x_ref[...] + gate_ref[0] * _dot(u_ref[...], w_ref[...])


def _m2_out(u, w_out, x2, gate, seq, tm, tn):
    t, d = x2.shape
    di = u.shape[1]
    tps = seq // tm
    return pl.pallas_call(
        _m2_out_kernel,
        grid=(t // tm, d // tn),
        in_specs=[pl.BlockSpec((tm, di), lambda i, j: (i, 0)),
                  pl.BlockSpec((di, tn), lambda i, j: (0, j)),
                  pl.BlockSpec((tm, tn), lambda i, j: (i, j)),
                  pl.BlockSpec((1, 1, tn), lambda i, j: (i // tps, 0, j))],
        out_specs=pl.BlockSpec((tm, tn), lambda i, j: (i, j)),
        out_shape=jax.ShapeDtypeStruct((t, d), F32),
        compiler_params=_params("arbitrary", "arbitrary"),
        name="m2_out_proj",
    )(u, w_out, x2, gate)


def _row_tile(seq, want):
    tm = min(want, seq)
    assert seq % tm == 0 and tm % HALO_ROWS == 0
    return tm


def _tiles(seq):
    return dict(tm=_row_tile(seq, 1024), tm_conv=_row_tile(seq, 512), tn=512, tn_wide=1024, tf=512)


def kernel(x, c, ada_w, ada_b, norm_g, final_g, sc_in_w, sc_conv_w, sc_out_w, m2_in_w, m2_conv_w, m2_conv_b,
           m2_dt_bias, m2_a_log, m2_d, m2_norm_g, m2_out_w, mlp_w1, mlp_w2):
    bsz, seq, d = x.shape
    t = bsz * seq
    d_inner = m2_norm_g.shape[1]
    n_heads = m2_d.shape[1]
    conv_dim = m2_conv_w.shape[1]
    d_state = (conv_dim - d_inner) // (2 * N_GROUPS)
    head_dim = d_inner // n_heads
    assert seq % CHUNK == 0
    ts = _tiles(seq)
    tm, tn, tf = ts["tm"], ts["tn"], ts["tf"]

    mods = _modulation(c, ada_w, ada_b)
    x2 = x.reshape(t, d)
    vec = lambda v: v.reshape(1, -1)

    shift, scale, gate = mods[0]
    bg, cv, (sc_out_bf, w1_bf, w2_bf) = _sc_in(x2, vec(norm_g[0, 0]), scale, shift, sc_in_w[0].astype(BF16),
                                               seq, tm, tn, [(sc_out_w, 0), (mlp_w1, 0), (mlp_w2, 0)])
    x2 = _sc_out(cv, bg, sc_conv_w[0].T, sc_out_bf, x2, gate, seq, ts["tm_conv"])
    shift, scale, gate = mods[1]
    x2, (m2_in_bf,) = _mlp(x2, vec(norm_g[0, 1]), scale, shift, gate, w1_bf, w2_bf,
                           vec(final_g), seq, tm, tf, final_norm=False, to_cast=[(m2_in_w, 0)])

    shift, scale, gate = mods[2]
    zxbc, dt_raw = _m2_in(x2, vec(norm_g[1, 0]), scale, shift, m2_in_bf, d_inner + conv_dim,
                          seq, tm, ts["tn_wide"])
    xbc = _conv_silu(zxbc, m2_conv_w[0].T, vec(m2_conv_b[0]), d_inner, seq, ts["tm_conv"], ts["tn_wide"])
    rr = n_heads // N_GROUPS
    a_cols, w_cols, row_t, diag_t, elast = _dt_prep(dt_raw, vec(m2_dt_bias[0]), vec(m2_a_log[0]), bsz, seq, rr)
    elast = jnp.repeat(elast.reshape(-1, 2, n_heads), head_dim, axis=2)
    dskip = jnp.repeat(m2_d[0], head_dim).reshape(1, d_inner)
    u, (m2_out_bf, w1_bf, w2_bf) = _ssd(xbc, zxbc, a_cols, w_cols, row_t, diag_t, elast, dskip, vec(m2_norm_g[0]),
                                        bsz, seq, d_inner, d_state, n_heads,
                                        [(m2_out_w, 0), (mlp_w1, 1), (mlp_w2, 1)])
    x2 = _m2_out(u, m2_out_bf, x2, gate, seq, tm, tn)
    shift, scale, gate = mods[3]
    x2, _ = _mlp(x2, vec(norm_g[1, 1]), scale, shift, gate, w1_bf, w2_bf,
                 vec(final_g), seq, tm, tf, final_norm=True)
    return x2.reshape(bsz, seq, d)
```

```python
import functools

import numpy as np
import jax
import jax.numpy as jnp
from jax import lax
from jax.experimental import pallas as pl
from jax.experimental.pallas import tpu as pltpu

N_GROUPS = 8
CHUNK = 128
DT_CHUNKS_PER_STEP = 4
SSD_CHUNKS_PER_STEP = 16
EPS = 1e-6
LOG2_E = 1.4426950408889634
SUBLANES = 8
LANES = 128
HALO_ROWS = 16
VMEM_LIMIT_BYTES = 56 * 1024 * 1024

F32 = jnp.float32
BF16 = jnp.bfloat16


def _params(*sem):
    return pltpu.CompilerParams(dimension_semantics=sem, vmem_limit_bytes=VMEM_LIMIT_BYTES)


def _dot(a, b):
    return jnp.dot(a, b, preferred_element_type=F32)


def _silu_of_half(half):
    return half * jnp.tanh(half) + half


def _silu(x):
    return _silu_of_half(0.5 * x)


def _rms(x, g):
    ms = jnp.mean(x * x, axis=-1, keepdims=True)
    return (x * lax.rsqrt(ms + EPS)) * g


def _modnorm(x, g, scale, shift):
    return _rms(x, g) * (1.0 + scale) + shift


def _conv3(main, prev_row, next_row, w):
    tm, width = main.shape
    core = w[0:1, :] * pltpu.roll(main, 1, 0) + w[1:2, :] * main + w[2:3, :] * pltpu.roll(main, tm - 1, 0)
    row = lax.broadcasted_iota(jnp.int32, (SUBLANES, width), 0)
    first = jnp.where(row == 0, w[0:1, :] * (prev_row - main[tm - 1:tm, :]), 0.0)
    last = jnp.where(row == SUBLANES - 1, w[2:3, :] * (next_row - main[0:1, :]), 0.0)
    return jnp.concatenate([core[:SUBLANES] + first, core[SUBLANES:tm - SUBLANES], core[tm - SUBLANES:] + last],
                           axis=0)


def _halo_rows(prev_ref, next_ref, tiles_per_seq):
    i = pl.program_id(0) % tiles_per_seq
    prev = prev_ref[...].astype(F32)[HALO_ROWS - 1:HALO_ROWS, :]
    nxt = next_ref[...].astype(F32)[0:1, :]
    prev = jnp.where(i == 0, 0.0, prev)
    nxt = jnp.where(i == tiles_per_seq - 1, 0.0, nxt)
    return prev, nxt


def _with_casts(body, n_in, n_out, n_cast):
    def kern(*refs):
        ins, rest = refs[:n_in], refs[n_in:]
        cast_in, rest = rest[:n_cast], rest[n_cast:]
        outs, rest = rest[:n_out], rest[n_out:]
        cast_out, scratch = rest[:n_cast], rest[n_cast:]
        for src, dst in zip(cast_in, cast_out):
            dst[...] = src[...].astype(BF16)
        body(*ins, *outs, *scratch)
    return kern


def _cast_specs(stacked, layer, n_steps, lin):
    _, r, c = stacked.shape
    nb = n_steps
    while r % nb or (r // nb) % HALO_ROWS:
        nb //= 2
    src = pl.BlockSpec((None, r // nb, c), lambda *idx: (layer, lin(*idx) * nb // n_steps, 0))
    dst = pl.BlockSpec((r // nb, c), lambda *idx: (lin(*idx) * nb // n_steps, 0))
    return src, dst, jax.ShapeDtypeStruct((r, c), BF16)


def _cast_args(to_cast, n_steps, lin):
    specs = [_cast_specs(w, layer, n_steps, lin) for w, layer in to_cast]
    return [s[0] for s in specs], [s[1] for s in specs], [s[2] for s in specs]


def _mod_kernel(c_ref, w_ref, b_ref, o_ref, cond_scr):
    @pl.when((pl.program_id(0) == 0) & (pl.program_id(1) == 0))
    def _():
        cond_scr[...] = _silu(c_ref[...])

    for slab in range(w_ref.shape[2] // LANES):
        cols = slice(slab * LANES, (slab + 1) * LANES)
        w = w_ref[0, :, cols]
        for b in range(c_ref.shape[0]):
            o_ref[0, b:b + 1, cols] = jnp.sum(w * cond_scr[b], axis=0, keepdims=True) + b_ref[0, :, cols]


def _modulation(c, ada_w, ada_b):
    bsz, d = c.shape
    n_mod = ada_w.shape[0] * ada_w.shape[1]
    w = ada_w.reshape(n_mod, d, 3 * d)
    b = ada_b.reshape(n_mod, 1, 3 * d)
    c_rep = jnp.broadcast_to(c[:, :, None], (bsz, d, LANES))
    tn = 1536
    out = pl.pallas_call(
        _mod_kernel,
        grid=(n_mod, 3 * d // tn),
        in_specs=[pl.BlockSpec((bsz, d, LANES), lambda m, j: (0, 0, 0)),
                  pl.BlockSpec((1, d, tn), lambda m, j: (m, 0, j)),
                  pl.BlockSpec((1, 1, tn), lambda m, j: (m, 0, j))],
        out_specs=pl.BlockSpec((1, bsz, tn), lambda m, j: (m, 0, j)),
        out_shape=jax.ShapeDtypeStruct((n_mod, bsz, 3 * d), F32),
        scratch_shapes=[pltpu.VMEM((bsz, d, LANES), F32)],
        compiler_params=_params("arbitrary", "arbitrary"),
        name="modulation",
    )(c_rep, w, b)
    out = out.reshape(n_mod, bsz, 3, 1, d)
    return [(out[m, :, 0], out[m, :, 1], out[m, :, 2]) for m in range(n_mod)]


def _sc_in_kernel(x_ref, g_ref, sc_ref, sh_ref, wb_ref, wc_ref, wv_ref, b_out, cv_out, h_scr):
    def project(h):
        b_out[...] = _dot(h, wb_ref[...]).astype(BF16)
        cv_out[...] = (_dot(h, wc_ref[...]) * _dot(h, wv_ref[...])).astype(BF16)

    @pl.when(pl.program_id(1) == 0)
    def _():
        h = _modnorm(x_ref[...], g_ref[...], sc_ref[0], sh_ref[0]).astype(BF16)
        h_scr[...] = h
        project(h)

    @pl.when(pl.program_id(1) != 0)
    def _():
        project(h_scr[...])


def _sc_in(x2, g, scale, shift, w_in, seq, tm, tn, to_cast):
    t, d = x2.shape
    tps = seq // tm
    nb = d // tn
    grid = (t // tm, nb)
    cast_in, cast_out, cast_shape = _cast_args(to_cast, grid[0] * grid[1], lambda i, j: i * nb + j)
    row_spec = pl.BlockSpec((tm, d), lambda i, j: (i, 0))
    vec_spec = pl.BlockSpec((1, d), lambda i, j: (0, 0))
    mod_spec = pl.BlockSpec((1, 1, d), lambda i, j: (i // tps, 0, 0))
    out_spec = pl.BlockSpec((tm, tn), lambda i, j: (i, j))
    outs = pl.pallas_call(
        _with_casts(_sc_in_kernel, 7, 2, len(to_cast)),
        grid=grid,
        in_specs=[row_spec, vec_spec, mod_spec, mod_spec,
                  pl.BlockSpec((d, tn), lambda i, j: (0, j)),
                  pl.BlockSpec((d, tn), lambda i, j: (0, j + nb)),
                  pl.BlockSpec((d, tn), lambda i, j: (0, j + 2 * nb))] + cast_in,
        out_specs=[out_spec, out_spec] + cast_out,
        out_shape=[jax.ShapeDtypeStruct((t, d), BF16), jax.ShapeDtypeStruct((t, d), BF16)] + cast_shape,
        scratch_shapes=[pltpu.VMEM((tm, d), BF16)],
        compiler_params=_params("arbitrary", "arbitrary"),
        name="sc_in_proj",
    )(x2, g, scale, shift, w_in, w_in, w_in, *[w for w, _ in to_cast])
    return outs[0], outs[1], outs[2:]


def _sc_out_kernel(cv_ref, cvp_ref, cvn_ref, bg_ref, cw_ref, w_ref, x_ref, gate_ref, o_ref, *, tiles_per_seq):
    prev, nxt = _halo_rows(cvp_ref, cvn_ref, tiles_per_seq)
    u = _conv3(cv_ref[...].astype(F32), prev, nxt, cw_ref[...])
    u = (u * bg_ref[...].astype(F32)).astype(BF16)
    o_ref[...] = x_ref[...] + gate_ref[0] * _dot(u, w_ref[...])


def _sc_out(cv, bg, conv_w, w_out, x2, gate, seq, tm):
    t, d = x2.shape
    tps = seq // tm
    hb = tm // HALO_ROWS
    last = t // HALO_ROWS - 1
    row_spec = pl.BlockSpec((tm, d), lambda i: (i, 0))
    return pl.pallas_call(
        functools.partial(_sc_out_kernel, tiles_per_seq=tps),
        grid=(t // tm,),
        in_specs=[row_spec,
                  pl.BlockSpec((HALO_ROWS, d), lambda i: (jnp.maximum(i * hb - 1, 0), 0)),
                  pl.BlockSpec((HALO_ROWS, d), lambda i: (jnp.minimum((i + 1) * hb, last), 0)),
                  row_spec,
                  pl.BlockSpec((3, d), lambda i: (0, 0)),
                  pl.BlockSpec((d, d), lambda i: (0, 0)),
                  row_spec,
                  pl.BlockSpec((1, 1, d), lambda i: (i // tps, 0, 0))],
        out_specs=row_spec,
        out_shape=jax.ShapeDtypeStruct((t, d), F32),
        compiler_params=_params("arbitrary"),
        name="sc_conv_out_proj",
    )(cv, cv, cv, bg, conv_w, w_out, x2, gate)


def _mlp_kernel(x_ref, g_ref, sc_ref, sh_ref, gate_ref, w1_ref, w2_ref, fg_ref, o_ref, h_scr, *, final_norm):
    k = pl.program_id(1)

    def hidden_chunk(h):
        a = jnp.maximum(_dot(h, w1_ref[...]), 0.0)
        return _dot((a * a).astype(BF16), w2_ref[...])

    @pl.when(k == 0)
    def _():
        h = _modnorm(x_ref[...], g_ref[...], sc_ref[0], sh_ref[0]).astype(BF16)
        h_scr[...] = h
        o_ref[...] = hidden_chunk(h)

    last = pl.num_programs(1) - 1

    @pl.when((k != 0) & (k != last))
    def _():
        o_ref[...] += hidden_chunk(h_scr[...])

    @pl.when(k == last)
    def _():
        y = x_ref[...] + gate_ref[0] * (o_ref[...] + hidden_chunk(h_scr[...]))
        if final_norm:
            y = _rms(y, fg_ref[...])
        o_ref[...] = y


def _mlp(x2, g, scale, shift, gate, w1, w2, final_g, seq, tm, tf, final_norm, to_cast=()):
    t, d = x2.shape
    ff = w1.shape[1]
    tps = seq // tm
    nk = ff // tf
    assert nk >= 2
    grid = (t // tm, nk)
    cast_in, cast_out, cast_shape = _cast_args(to_cast, grid[0] * grid[1], lambda i, k: i * nk + k)
    row_spec = pl.BlockSpec((tm, d), lambda i, k: (i, 0))
    vec_spec = pl.BlockSpec((1, d), lambda i, k: (0, 0))
    mod_spec = pl.BlockSpec((1, 1, d), lambda i, k: (i // tps, 0, 0))
    outs = pl.pallas_call(
        _with_casts(functools.partial(_mlp_kernel, final_norm=final_norm), 8, 1, len(to_cast)),
        grid=grid,
        in_specs=[row_spec, vec_spec, mod_spec, mod_spec, mod_spec,
                  pl.BlockSpec((d, tf), lambda i, k: (0, k)),
                  pl.BlockSpec((tf, d), lambda i, k: (k, 0)),
                  vec_spec] + cast_in,
        out_specs=[row_spec] + cast_out,
        out_shape=[jax.ShapeDtypeStruct((t, d), F32)] + cast_shape,
        scratch_shapes=[pltpu.VMEM((tm, d), BF16)],
        compiler_params=_params("arbitrary", "arbitrary"),
        name="mlp_final" if final_norm else "mlp",
    )(x2, g, scale, shift, gate, w1, w2, final_g, *[w for w, _ in to_cast])
    return outs[0], outs[1:]


def _m2_in_kernel(x_ref, xp_ref, xn_ref, g_ref, sc_ref, sh_ref, w_ref, wdt_ref, cw_ref, cb_ref,
                  o_ref, dt_ref, h_scr, *, tiles_per_seq, n_plain):
    j = pl.program_id(1)
    tm = x_ref.shape[0]

    @pl.when(j == 0)
    def _():
        h = _modnorm(x_ref[...], g_ref[...], sc_ref[0], sh_ref[0]).astype(BF16)
        h_scr[:tm, :] = h
        halo = jnp.concatenate([xp_ref[...], xn_ref[...]], axis=0)
        h_scr[tm:, :] = _modnorm(halo, g_ref[...], sc_ref[0], sh_ref[0]).astype(BF16)
        dt_ref[...] = _dot(h, wdt_ref[...])
        o_ref[...] = _dot(h, w_ref[...]).astype(BF16)

    @pl.when((j != 0) & (j < n_plain))
    def _():
        o_ref[...] = _dot(h_scr[:tm, :], w_ref[...]).astype(BF16)

    @pl.when(j >= n_plain)
    def _():
        i = pl.program_id(0) % tiles_per_seq
        proj = _dot(h_scr[...], w_ref[...])
        prev = jnp.where(i == 0, 0.0, proj[tm + SUBLANES - 1:tm + SUBLANES, :])
        nxt = jnp.where(i == tiles_per_seq - 1, 0.0, proj[tm + SUBLANES:tm + SUBLANES + 1, :])
        half = _conv3(proj[:tm, :], prev, nxt, 0.5 * cw_ref[...]) + 0.5 * cb_ref[...]
        o_ref[...] = _silu_of_half(half).astype(BF16)


def _m2_in(x2, g, scale, shift, w_in, conv_w, conv_b, n_plain_cols, seq, tm, tn):
    t, d = x2.shape
    n = n_plain_cols + conv_w.shape[1]
    ndt = w_in.shape[1] - n
    assert n_plain_cols % tn == 0 and n % tn == 0 and n % ndt == 0
    n_plain = n_plain_cols // tn
    tps = seq // tm
    hb = tm // SUBLANES
    last = t // SUBLANES - 1
    mod_spec = pl.BlockSpec((1, 1, d), lambda i, j: (i // tps, 0, 0))
    return pl.pallas_call(
        functools.partial(_m2_in_kernel, tiles_per_seq=tps, n_plain=n_plain),
        grid=(t // tm, n // tn),
        in_specs=[pl.BlockSpec((tm, d), lambda i, j: (i, 0)),
                  pl.BlockSpec((SUBLANES, d), lambda i, j: (jnp.maximum(i * hb - 1, 0), 0)),
                  pl.BlockSpec((SUBLANES, d), lambda i, j: (jnp.minimum((i + 1) * hb, last), 0)),
                  pl.BlockSpec((1, d), lambda i, j: (0, 0)),
                  mod_spec, mod_spec,
                  pl.BlockSpec((d, tn), lambda i, j: (0, j)),
                  pl.BlockSpec((d, ndt), lambda i, j: (0, n // ndt)),
                  pl.BlockSpec((3, tn), lambda i, j: (0, jnp.maximum(j - n_plain, 0))),
                  pl.BlockSpec((1, tn), lambda i, j: (0, jnp.maximum(j - n_plain, 0)))],
        out_specs=[pl.BlockSpec((tm, tn), lambda i, j: (i, j)),
                   pl.BlockSpec((tm, ndt), lambda i, j: (i, 0))],
        out_shape=[jax.ShapeDtypeStruct((t, n), BF16), jax.ShapeDtypeStruct((t, ndt), F32)],
        scratch_shapes=[pltpu.VMEM((tm + 2 * SUBLANES, d), BF16)],
        compiler_params=_params("arbitrary", "arbitrary"),
        name="m2_in_proj",
    )(x2, x2, x2, g, scale, shift, w_in, w_in, conv_w, conv_b)


N_PIECES = 3
GROUP_LANES = 128


def _piece_base(heads_per_group):
    w = N_PIECES * heads_per_group
    return {"cum_f": 0, "cum_b": w, "ecum_f": 2 * w, "ecum_b": 3 * w, "wend_f": 4 * w}


def _perm_matrices(n_heads, heads_per_group):
    rr = heads_per_group
    base = _piece_base(rr)
    nat = 2 * n_heads
    perm_a = np.zeros((3 * N_PIECES * nat, N_GROUPS * GROUP_LANES), np.float32)
    perm_w = np.zeros((N_PIECES * nat, N_GROUPS * GROUP_LANES), np.float32)
    for piece in range(N_PIECES):
        for d in range(2):
            for h in range(n_heads):
                g, r = divmod(h, rr)
                lane = piece * nat + d * n_heads + h
                col = g * GROUP_LANES + piece * rr + r
                perm_a[0 * N_PIECES * nat + lane, col + (base["cum_f"] if d == 0 else base["cum_b"])] = 1
                perm_a[1 * N_PIECES * nat + lane, col + (base["ecum_f"] if d == 0 else base["ecum_b"])] = 1
                if d == 0:
                    perm_a[2 * N_PIECES * nat + lane, col + base["wend_f"]] = 1
                else:
                    perm_w[lane, col] = 1
    return jnp.asarray(perm_a, BF16), jnp.asarray(perm_w, BF16)


def _select_matrices(heads_per_group, head_dim, q):
    rr = heads_per_group
    base = _piece_base(rr)
    gw = rr * head_dim
    sel_c = np.zeros((GROUP_LANES, 2 * rr * q), np.float32)
    sel_e = np.zeros((GROUP_LANES, 3 * gw), np.float32)
    sel_w = np.zeros((GROUP_LANES, gw), np.float32)
    for piece in range(N_PIECES):
        for r in range(rr):
            k = piece * rr + r
            sel_c[base["cum_f"] + k, r * q:(r + 1) * q] = 1
            sel_c[base["cum_b"] + k, (rr + r) * q:(rr + r + 1) * q] = 1
            sel_e[base["ecum_f"] + k, r * head_dim:(r + 1) * head_dim] = 1
            sel_e[base["ecum_b"] + k, gw + r * head_dim:gw + (r + 1) * head_dim] = 1
            sel_e[base["wend_f"] + k, 2 * gw + r * head_dim:2 * gw + (r + 1) * head_dim] = 1
            sel_w[k, r * head_dim:(r + 1) * head_dim] = 1
    return jnp.asarray(sel_c, BF16), jnp.asarray(sel_e, BF16), jnp.asarray(sel_w, BF16)


def _bf16_pieces(v):
    hi = v.astype(BF16)
    rem = v - hi.astype(F32)
    mid = rem.astype(BF16)
    lo = (rem - mid.astype(F32)).astype(BF16)
    return [hi, mid, lo]


def _dt_kernel(raw_ref, bias_ref, alog_ref, perm_a_ref, perm_w_ref,
               a_ref, w_ref, rowt_ref, diagt_ref, elast_ref, *, n_heads):
    q = CHUNK
    w = raw_ref.shape[1]
    r = lax.broadcasted_iota(jnp.int32, (q, q), 0)
    c = lax.broadcasted_iota(jnp.int32, (q, q), 1)
    lower = jnp.where(c <= r, 1.0, 0.0).astype(BF16)
    upper = jnp.where(c >= r, 1.0, 0.0).astype(BF16)
    fwd = lax.broadcasted_iota(jnp.int32, (q, w), 1) < n_heads
    neg_a2 = jnp.exp(alog_ref[...]) * LOG2_E
    lhs_a, lhs_w = [], []
    for k in range(raw_ref.shape[0] // q):
        rows = slice(k * q, (k + 1) * q)
        v = raw_ref[rows, :] + bias_ref[...]
        dt = jnp.maximum(v, 0.0) + jnp.log1p(jnp.exp(-jnp.abs(v)))
        a2 = -(dt * neg_a2)
        pieces = jnp.concatenate(_bf16_pieces(a2), axis=1)
        pre = _dot(lower, pieces)
        suf = _dot(upper, pieces)
        cum = jnp.where(fwd, pre[:, :w] + pre[:, w:2 * w] + pre[:, 2 * w:],
                        suf[:, :w] + suf[:, w:2 * w] + suf[:, 2 * w:])
        last = jnp.where(fwd[0:1, :], cum[q - 1:q, :], cum[0:1, :])
        wend = _bf16_pieces(jnp.exp2(last - cum) * dt)
        lhs_a.append(jnp.concatenate(_bf16_pieces(cum) + _bf16_pieces(jnp.exp2(cum)) + wend, axis=1))
        lhs_w.append(jnp.concatenate(wend, axis=1))
        rowt_ref[0, :, rows] = (cum - jnp.log2(dt)).T
        diagt_ref[0, :, rows] = jnp.log2(dt + pltpu.roll(dt, n_heads, 1)).T
        elast_ref[k] = jnp.exp2(last)
    a_ref[...] = _dot(jnp.concatenate(lhs_a, axis=0), perm_a_ref[...]).astype(BF16)
    w_ref[...] = _dot(jnp.concatenate(lhs_w, axis=0), perm_w_ref[...]).astype(BF16)


def _dt_prep(dt_raw, dt_bias, a_log, bsz, seq, heads_per_group):
    t, w = dt_raw.shape
    q = CHUNK
    nc = seq // q
    perm_a, perm_w = _perm_matrices(w // 2, heads_per_group)
    gl = N_GROUPS * GROUP_LANES
    per_step = DT_CHUNKS_PER_STEP
    assert nc % per_step == 0
    rows = per_step * q
    nblk = nc // per_step
    col_spec = pl.BlockSpec((rows, w), lambda i: (i, 0))
    vec_spec = pl.BlockSpec((1, w), lambda i: (0, 0))
    grp_spec = pl.BlockSpec((rows, gl), lambda i: (i, 0))
    return pl.pallas_call(
        functools.partial(_dt_kernel, n_heads=w // 2),
        grid=(t // rows,),
        in_specs=[col_spec, vec_spec, vec_spec,
                  pl.BlockSpec(perm_a.shape, lambda i: (0, 0)),
                  pl.BlockSpec(perm_w.shape, lambda i: (0, 0))],
        out_specs=[grp_spec, grp_spec,
                   pl.BlockSpec((1, w, rows), lambda i: (i // nblk, 0, i % nblk)),
                   pl.BlockSpec((1, w, rows), lambda i: (i // nblk, 0, i % nblk)),
                   pl.BlockSpec((per_step, 1, w), lambda i: (i, 0, 0))],
        out_shape=[jax.ShapeDtypeStruct((t, gl), BF16), jax.ShapeDtypeStruct((t, gl), BF16),
                   jax.ShapeDtypeStruct((bsz, w, seq), F32), jax.ShapeDtypeStruct((bsz, w, seq), F32),
                   jax.ShapeDtypeStruct((t // q, 1, w), F32)],
        compiler_params=_params("arbitrary"),
        name="m2_dt_prep",
    )(dt_raw, dt_bias, a_log, perm_a, perm_w)


def _ssd_backward_block(blk, x_ref, b_ref, w_ref, selw_ref, el_ref, stb, sb_scr, *, q, per_step):
    ks = list(reversed(range(per_step)))
    rows = [slice(k * q, (k + 1) * q) for k in range(per_step)]
    wexp = {k: _dot(w_ref[rows[k], :], selw_ref[...]) for k in ks}
    upd = {}
    for k in ks:
        xw = (x_ref[rows[k], :].astype(F32) * wexp[k]).astype(BF16)
        bt = b_ref[rows[k], :].astype(F32).T.astype(BF16)
        upd[k] = _dot(bt, xw)
    st = stb[...]
    for k in ks:
        sb_scr[blk * per_step + k] = st.astype(BF16)
        st = st * el_ref[0, k, 1:2, :] + upd[k]
    stb[...] = st


def _ssd_forward_block(blk, x_ref, b_ref, c_ref, z_ref, a_ref, selc_ref, sele_ref, rtf_ref, rtb_ref, dg_ref,
                       el_ref, dsk_ref, ng_ref, u_ref, stf, sb_scr, *, q, per_step, heads_per_group, head_dim):
    rr = heads_per_group
    gw = rr * head_dim
    ks = list(range(per_step))
    rows = [slice(k * q, (k + 1) * q) for k in ks]
    t_idx = lax.broadcasted_iota(jnp.int32, (q, q), 0)
    s_idx = lax.broadcasted_iota(jnp.int32, (q, q), 1)
    below = t_idx > s_idx
    above = t_idx < s_idx
    lane_head = lax.broadcasted_iota(jnp.int32, (q, 256), 1) // head_dim
    n_pairs = rr // 2
    heads_per_tile = 256 // head_dim

    scores, col_f, col_b, exp_f, exp_b, exp_w = {}, {}, {}, {}, {}, {}
    for k in ks:
        a = a_ref[rows[k], :]
        scores[k] = lax.dot_general(c_ref[rows[k], :], b_ref[rows[k], :], (((1,), (1,)), ((), ())),
                                    preferred_element_type=F32)
        for pr in range(n_pairs):
            col_f[k, pr] = _dot(a, selc_ref[:, 2 * pr * q:(2 * pr + 2) * q])
            col_b[k, pr] = _dot(a, selc_ref[:, (rr + 2 * pr) * q:(rr + 2 * pr + 2) * q])
        exp_f[k] = _dot(a, sele_ref[:, 0:gw])
        exp_b[k] = _dot(a, sele_ref[:, gw:2 * gw])
        exp_w[k] = _dot(a, sele_ref[:, 2 * gw:3 * gw])

    y, upd = {}, {}
    for k in ks:
        x = x_ref[rows[k], :]
        rtf, rtb, dg = rtf_ref[0, :, rows[k]], rtb_ref[0, :, rows[k]], dg_ref[0, :, rows[k]]
        y_parts = []
        for tile in range(gw // 256):
            xt = x[:, tile * 256:(tile + 1) * 256]
            acc = None
            for pair in range(heads_per_tile // 2):
                pr = tile * (heads_per_tile // 2) + pair
                ms, xs = [], []
                for j in range(2):
                    h = 2 * pr + j
                    cols = slice(j * q, (j + 1) * q)
                    seg = jnp.where(below, col_f[k, pr][:, cols] - rtf[h:h + 1, :],
                                    jnp.where(above, col_b[k, pr][:, cols] - rtb[h:h + 1, :], dg[h:h + 1, :]))
                    ms.append((scores[k] * jnp.exp2(seg)).astype(BF16))
                    xs.append(jnp.where(lane_head == 2 * pair + j, xt, jnp.zeros_like(xt)))
                part = _dot(jnp.concatenate(ms, axis=1), jnp.concatenate(xs, axis=0))
                acc = part if acc is None else acc + part
            y_parts.append(acc)
        xf32 = x.astype(F32)
        y[k] = (jnp.concatenate(y_parts, axis=1) + dsk_ref[...] * xf32
                + exp_b[k] * _dot(c_ref[rows[k], :], sb_scr[blk * per_step + k]))
        xw = (xf32 * exp_w[k]).astype(BF16)
        bt = b_ref[rows[k], :].astype(F32).T.astype(BF16)
        upd[k] = _dot(bt, xw)

    st = stf[...]
    sts = {}
    for k in ks:
        sts[k] = st.astype(BF16)
        st = st * el_ref[0, k, 0:1, :] + upd[k]
    stf[...] = st
    for k in ks:
        yk = y[k] + exp_f[k] * _dot(c_ref[rows[k], :], sts[k])
        u = yk * _silu(z_ref[rows[k], :].astype(F32))
        u_ref[rows[k], :] = _rms(u, ng_ref[...]).astype(BF16)


def _ssd_kernel(x_ref, b_ref, c_ref, z_ref, a_ref, w_ref, selc_ref, sele_ref, selw_ref,
                rtf_ref, rtb_ref, dg_ref, el_ref, dsk_ref, ng_ref,
                u_ref, stf, stb, sb_scr, *, heads_per_group, head_dim, per_step):
    q = x_ref.shape[0] // per_step
    phase = pl.program_id(2)
    step = pl.program_id(3)
    nblk = pl.num_programs(3)

    @pl.when(phase == 0)
    def _():
        @pl.when(step == 0)
        def _():
            stb[...] = jnp.zeros_like(stb)

        _ssd_backward_block(nblk - 1 - step, x_ref, b_ref, w_ref, selw_ref, el_ref, stb, sb_scr,
                            q=q, per_step=per_step)

    @pl.when(phase == 1)
    def _():
        @pl.when(step == 0)
        def _():
            stf[...] = jnp.zeros_like(stf)

        _ssd_forward_block(step, x_ref, b_ref, c_ref, z_ref, a_ref, selc_ref, sele_ref, rtf_ref, rtb_ref, dg_ref,
                           el_ref, dsk_ref, ng_ref, u_ref, stf, sb_scr, q=q, per_step=per_step,
                           heads_per_group=heads_per_group, head_dim=head_dim)


def _ssd(zxbc, a_cols, w_cols, row_t, diag_t, elast, dskip, norm_g, bsz, seq, d_inner, d_state, n_heads,
         to_cast):
    t = zxbc.shape[0]
    q = CHUNK
    nc = seq // q
    per_step = SSD_CHUNKS_PER_STEP
    assert nc % per_step == 0
    nblk = nc // per_step
    rows = per_step * q
    g = N_GROUPS
    rr = n_heads // g
    p = d_inner // n_heads
    gw = rr * p
    assert p == 64 and d_state == 128 and q == 128 and gw % 256 == 0 and rr % 2 == 0
    assert 5 * N_PIECES * rr <= GROUP_LANES and d_inner // g == gw
    xoff = d_inner // gw
    boff = 2 * d_inner // d_state
    sel_c, sel_e, sel_w = _select_matrices(rr, p, q)
    elast = elast.reshape(bsz * nblk, per_step, 2, d_inner)

    def both(b, gi, ph, s):
        return b * nblk + ph * s + (1 - ph) * (nblk - 1 - s)

    def fwd(b, gi, ph, s):
        return b * nblk + ph * s

    def bwd(b, gi, ph, s):
        return b * nblk + (1 - ph) * (nblk - 1 - s)

    const = lambda arr: pl.BlockSpec(arr.shape, lambda b, gi, ph, s: (0, 0))
    grid = (bsz, g, 2, nblk)
    cast_in, cast_out, cast_shape = _cast_args(to_cast, bsz * g * 2 * nblk,
                                               lambda b, gi, ph, s: ((b * g + gi) * 2 + ph) * nblk + s)
    outs = pl.pallas_call(
        _with_casts(functools.partial(_ssd_kernel, heads_per_group=rr, head_dim=p, per_step=per_step),
                    15, 1, len(to_cast)),
        grid=grid,
        in_specs=[pl.BlockSpec((rows, gw), lambda b, gi, ph, s: (both(b, gi, ph, s), xoff + gi)),
                  pl.BlockSpec((rows, d_state), lambda b, gi, ph, s: (both(b, gi, ph, s), boff + gi)),
                  pl.BlockSpec((rows, d_state), lambda b, gi, ph, s: (fwd(b, gi, ph, s), boff + g + gi)),
                  pl.BlockSpec((rows, gw), lambda b, gi, ph, s: (fwd(b, gi, ph, s), gi)),
                  pl.BlockSpec((rows, GROUP_LANES), lambda b, gi, ph, s: (fwd(b, gi, ph, s), gi)),
                  pl.BlockSpec((rows, GROUP_LANES), lambda b, gi, ph, s: (bwd(b, gi, ph, s), gi)),
                  const(sel_c), const(sel_e), const(sel_w),
                  pl.BlockSpec((1, rr, rows), lambda b, gi, ph, s: (b, gi, ph * s)),
                  pl.BlockSpec((1, rr, rows), lambda b, gi, ph, s: (b, g + gi, ph * s)),
                  pl.BlockSpec((1, rr, rows), lambda b, gi, ph, s: (b, gi, ph * s)),
                  pl.BlockSpec((1, per_step, 2, gw), lambda b, gi, ph, s: (both(b, gi, ph, s), 0, 0, gi)),
                  pl.BlockSpec((1, gw), lambda b, gi, ph, s: (0, gi)),
                  pl.BlockSpec((1, gw), lambda b, gi, ph, s: (0, gi))] + cast_in,
        out_specs=[pl.BlockSpec((rows, gw), lambda b, gi, ph, s: (fwd(b, gi, ph, s), gi))] + cast_out,
        out_shape=[jax.ShapeDtypeStruct((t, d_inner), BF16)] + cast_shape,
        scratch_shapes=[pltpu.VMEM((d_state, gw), F32), pltpu.VMEM((d_state, gw), F32),
                        pltpu.VMEM((nc, d_state, gw), BF16)],
        compiler_params=_params("arbitrary", "arbitrary", "arbitrary", "arbitrary"),
        name="m2_ssd",
    )(zxbc, zxbc, zxbc, zxbc, a_cols, w_cols, sel_c, sel_e, sel_w, row_t, row_t, diag_t, elast, dskip, norm_g,
      *[w for w, _ in to_cast])
    return outs[0], outs[1:]


def _m2_out_kernel(u_ref, w_ref, x_ref, gate_ref, o_ref):
    o_ref[...] = x_ref[...] + gate_ref[0] * _dot(u_ref[...], w_ref[...])


def _m2_out(u, w_out, x2, gate, seq, tm, tn):
    t, d = x2.shape
    di = u.shape[1]
    tps = seq // tm
    return pl.pallas_call(
        _m2_out_kernel,
        grid=(t // tm, d // tn),
        in_specs=[pl.BlockSpec((tm, di), lambda i, j: (i, 0)),
                  pl.BlockSpec((di, tn), lambda i, j: (0, j)),
                  pl.BlockSpec((tm, tn), lambda i, j: (i, j)),
                  pl.BlockSpec((1, 1, tn), lambda i, j: (i // tps, 0, j))],
        out_specs=pl.BlockSpec((tm, tn), lambda i, j: (i, j)),
        out_shape=jax.ShapeDtypeStruct((t, d), F32),
        compiler_params=_params("arbitrary", "arbitrary"),
        name="m2_out_proj",
    )(u, w_out, x2, gate)


def _row_tile(seq, want):
    tm = min(want, seq)
    assert seq % tm == 0 and tm % HALO_ROWS == 0
    return tm


def _tiles(seq):
    return dict(tm=_row_tile(seq, 1024), tm_conv=_row_tile(seq, 512), tn=512, tn_wide=1024, tf=512)


def kernel(x, c, ada_w, ada_b, norm_g, final_g, sc_in_w, sc_conv_w, sc_out_w, m2_in_w, m2_conv_w, m2_conv_b,
           m2_dt_bias, m2_a_log, m2_d, m2_norm_g, m2_out_w, mlp_w1, mlp_w2):
    bsz, seq, d = x.shape
    t = bsz * seq
    d_inner = m2_norm_g.shape[1]
    n_heads = m2_d.shape[1]
    conv_dim = m2_conv_w.shape[1]
    d_state = (conv_dim - d_inner) // (2 * N_GROUPS)
    head_dim = d_inner // n_heads
    assert seq % CHUNK == 0
    ts = _tiles(seq)
    tm, tn, tf = ts["tm"], ts["tn"], ts["tf"]

    mods = _modulation(c, ada_w, ada_b)
    x2 = x.reshape(t, d)
    vec = lambda v: v.reshape(1, -1)

    shift, scale, gate = mods[0]
    bg, cv, (sc_out_bf, w1_bf, w2_bf) = _sc_in(x2, vec(norm_g[0, 0]), scale, shift, sc_in_w[0].astype(BF16),
                                               seq, tm, tn, [(sc_out_w, 0), (mlp_w1, 0), (mlp_w2, 0)])
    x2 = _sc_out(cv, bg, sc_conv_w[0].T, sc_out_bf, x2, gate, seq, ts["tm_conv"])
    shift, scale, gate = mods[1]
    x2, (m2_in_bf,) = _mlp(x2, vec(norm_g[0, 1]), scale, shift, gate, w1_bf, w2_bf,
                           vec(final_g), seq, tm, tf, final_norm=False, to_cast=[(m2_in_w, 0)])

    shift, scale, gate = mods[2]
    zxbc, dt_raw = _m2_in(x2, vec(norm_g[1, 0]), scale, shift, m2_in_bf, m2_conv_w[0].T, vec(m2_conv_b[0]),
                          d_inner, seq, tm, ts["tn_wide"])
    rr = n_heads // N_GROUPS
    a_cols, w_cols, row_t, diag_t, elast = _dt_prep(dt_raw, vec(m2_dt_bias[0]), vec(m2_a_log[0]), bsz, seq, rr)
    elast = jnp.repeat(elast.reshape(-1, 2, n_heads), head_dim, axis=2)
    dskip = jnp.repeat(m2_d[0], head_dim).reshape(1, d_inner)
    u, (m2_out_bf, w1_bf, w2_bf) = _ssd(zxbc, a_cols, w_cols, row_t, diag_t, elast, dskip, vec(m2_norm_g[0]),
                                        bsz, seq, d_inner, d_state, n_heads,
                                        [(m2_out_w, 0), (mlp_w1, 1), (mlp_w2, 1)])
    x2 = _m2_out(u, m2_out_bf, x2, gate, seq, tm, tn)
    shift, scale, gate = mods[3]
    x2, _ = _mlp(x2, vec(norm_g[1, 1]), scale, shift, gate, w1_bf, w2_bf,
                 vec(final_g), seq, tm, tf, final_norm=True)
    return x2.reshape(bsz, seq, d)
```

```python
import functools

import numpy as np
import jax
import jax.numpy as jnp
from jax import lax
from jax.experimental import pallas as pl
from jax.experimental.pallas import tpu as pltpu

N_GROUPS = 8
CHUNK = 128
MLP_EPILOGUE_SLABS = 4
DT_CHUNKS_PER_STEP = 8
SSD_CHUNKS_PER_STEP = 16
EPS = 1e-6
LOG2_E = 1.4426950408889634
SUBLANES = 8
LANES = 128
HALO_ROWS = 16
VMEM_LIMIT_BYTES = 62 * 1024 * 1024

F32 = jnp.float32
BF16 = jnp.bfloat16


def _params(*sem):
    return pltpu.CompilerParams(dimension_semantics=sem, vmem_limit_bytes=VMEM_LIMIT_BYTES)


def _dot(a, b):
    return jnp.dot(a, b, preferred_element_type=F32)


def _silu_of_half(half):
    return half * jnp.tanh(half) + half


def _silu(x):
    return _silu_of_half(0.5 * x)


def _rms(x, g):
    ms = jnp.mean(x * x, axis=-1, keepdims=True)
    return (x * lax.rsqrt(ms + EPS)) * g


def _modnorm(x, g, scale, shift):
    return _rms(x, g) * (1.0 + scale) + shift


def _conv3(main, prev_row, next_row, w):
    tm, width = main.shape
    core = w[0:1, :] * pltpu.roll(main, 1, 0) + w[1:2, :] * main + w[2:3, :] * pltpu.roll(main, tm - 1, 0)
    row = lax.broadcasted_iota(jnp.int32, (SUBLANES, width), 0)
    first = jnp.where(row == 0, w[0:1, :] * (prev_row - main[tm - 1:tm, :]), 0.0)
    last = jnp.where(row == SUBLANES - 1, w[2:3, :] * (next_row - main[0:1, :]), 0.0)
    return jnp.concatenate([core[:SUBLANES] + first, core[SUBLANES:tm - SUBLANES], core[tm - SUBLANES:] + last],
                           axis=0)


def _halo_rows(prev_ref, next_ref, tiles_per_seq):
    i = pl.program_id(0) % tiles_per_seq
    prev = prev_ref[...].astype(F32)[HALO_ROWS - 1:HALO_ROWS, :]
    nxt = next_ref[...].astype(F32)[0:1, :]
    prev = jnp.where(i == 0, 0.0, prev)
    nxt = jnp.where(i == tiles_per_seq - 1, 0.0, nxt)
    return prev, nxt


def _with_casts(body, n_in, n_out, n_cast):
    def kern(*refs):
        ins, rest = refs[:n_in], refs[n_in:]
        cast_in, rest = rest[:n_cast], rest[n_cast:]
        outs, rest = rest[:n_out], rest[n_out:]
        cast_out, scratch = rest[:n_cast], rest[n_cast:]
        for src, dst in zip(cast_in, cast_out):
            dst[...] = src[...].astype(BF16)
        body(*ins, *outs, *scratch)
    return kern


def _cast_specs(stacked, layer, n_steps, lin):
    _, r, c = stacked.shape
    nb = n_steps
    while r % nb or (r // nb) % HALO_ROWS:
        nb //= 2
    src = pl.BlockSpec((None, r // nb, c), lambda *idx: (layer, lin(*idx) * nb // n_steps, 0))
    dst = pl.BlockSpec((r // nb, c), lambda *idx: (lin(*idx) * nb // n_steps, 0))
    return src, dst, jax.ShapeDtypeStruct((r, c), BF16)


def _cast_args(to_cast, n_steps, lin):
    specs = [_cast_specs(w, layer, n_steps, lin) for w, layer in to_cast]
    return [s[0] for s in specs], [s[1] for s in specs], [s[2] for s in specs]


def _mod_kernel(c_ref, w_ref, b_ref, o_ref, cond_scr):
    @pl.when((pl.program_id(0) == 0) & (pl.program_id(1) == 0))
    def _():
        cond_scr[...] = _silu(c_ref[...])

    for slab in range(w_ref.shape[2] // LANES):
        cols = slice(slab * LANES, (slab + 1) * LANES)
        w = w_ref[0, :, cols]
        for b in range(c_ref.shape[0]):
            o_ref[0, b:b + 1, cols] = jnp.sum(w * cond_scr[b], axis=0, keepdims=True) + b_ref[0, :, cols]


def _modulation(c, ada_w, ada_b):
    bsz, d = c.shape
    n_mod = ada_w.shape[0] * ada_w.shape[1]
    w = ada_w.reshape(n_mod, d, 3 * d)
    b = ada_b.reshape(n_mod, 1, 3 * d)
    c_rep = jnp.broadcast_to(c[:, :, None], (bsz, d, LANES))
    tn = 1536
    out = pl.pallas_call(
        _mod_kernel,
        grid=(n_mod, 3 * d // tn),
        in_specs=[pl.BlockSpec((bsz, d, LANES), lambda m, j: (0, 0, 0)),
                  pl.BlockSpec((1, d, tn), lambda m, j: (m, 0, j)),
                  pl.BlockSpec((1, 1, tn), lambda m, j: (m, 0, j))],
        out_specs=pl.BlockSpec((1, bsz, tn), lambda m, j: (m, 0, j)),
        out_shape=jax.ShapeDtypeStruct((n_mod, bsz, 3 * d), F32),
        scratch_shapes=[pltpu.VMEM((bsz, d, LANES), F32)],
        compiler_params=_params("arbitrary", "arbitrary"),
        name="modulation",
    )(c_rep, w, b)
    out = out.reshape(n_mod, bsz, 3, 1, d)
    return [(out[m, :, 0], out[m, :, 1], out[m, :, 2]) for m in range(n_mod)]


def _sc_in_kernel(x_ref, g_ref, sc_ref, sh_ref, wb_ref, wc_ref, wv_ref, b_out, cv_out, h_scr):
    def project(h):
        b_out[...] = _dot(h, wb_ref[...]).astype(BF16)
        cv_out[...] = (_dot(h, wc_ref[...]) * _dot(h, wv_ref[...])).astype(BF16)

    @pl.when(pl.program_id(1) == 0)
    def _():
        h = _modnorm(x_ref[...], g_ref[...], sc_ref[0], sh_ref[0]).astype(BF16)
        h_scr[...] = h
        project(h)

    @pl.when(pl.program_id(1) != 0)
    def _():
        project(h_scr[...])


def _sc_in(x2, g, scale, shift, w_in, seq, tm, tn, to_cast):
    t, d = x2.shape
    tps = seq // tm
    nb = d // tn
    grid = (t // tm, nb)
    cast_in, cast_out, cast_shape = _cast_args(to_cast, grid[0] * grid[1], lambda i, j: i * nb + j)
    row_spec = pl.BlockSpec((tm, d), lambda i, j: (i, 0))
    vec_spec = pl.BlockSpec((1, d), lambda i, j: (0, 0))
    mod_spec = pl.BlockSpec((1, 1, d), lambda i, j: (i // tps, 0, 0))
    out_spec = pl.BlockSpec((tm, tn), lambda i, j: (i, j))
    outs = pl.pallas_call(
        _with_casts(_sc_in_kernel, 7, 2, len(to_cast)),
        grid=grid,
        in_specs=[row_spec, vec_spec, mod_spec, mod_spec,
                  pl.BlockSpec((d, tn), lambda i, j: (0, j)),
                  pl.BlockSpec((d, tn), lambda i, j: (0, j + nb)),
                  pl.BlockSpec((d, tn), lambda i, j: (0, j + 2 * nb))] + cast_in,
        out_specs=[out_spec, out_spec] + cast_out,
        out_shape=[jax.ShapeDtypeStruct((t, d), BF16), jax.ShapeDtypeStruct((t, d), BF16)] + cast_shape,
        scratch_shapes=[pltpu.VMEM((tm, d), BF16)],
        compiler_params=_params("arbitrary", "arbitrary"),
        name="sc_in_proj",
    )(x2, g, scale, shift, w_in, w_in, w_in, *[w for w, _ in to_cast])
    return outs[0], outs[1], outs[2:]


def _sc_out_kernel(cv_ref, cvp_ref, cvn_ref, bg_ref, cw_ref, w_ref, x_ref, gate_ref, o_ref, *, tiles_per_seq):
    prev, nxt = _halo_rows(cvp_ref, cvn_ref, tiles_per_seq)
    u = _conv3(cv_ref[...].astype(F32), prev, nxt, cw_ref[...])
    u = (u * bg_ref[...].astype(F32)).astype(BF16)
    o_ref[...] = x_ref[...] + gate_ref[0] * _dot(u, w_ref[...])


def _sc_out(cv, bg, conv_w, w_out, x2, gate, seq, tm):
    t, d = x2.shape
    tps = seq // tm
    hb = tm // HALO_ROWS
    last = t // HALO_ROWS - 1
    row_spec = pl.BlockSpec((tm, d), lambda i: (i, 0))
    return pl.pallas_call(
        functools.partial(_sc_out_kernel, tiles_per_seq=tps),
        grid=(t // tm,),
        in_specs=[row_spec,
                  pl.BlockSpec((HALO_ROWS, d), lambda i: (jnp.maximum(i * hb - 1, 0), 0)),
                  pl.BlockSpec((HALO_ROWS, d), lambda i: (jnp.minimum((i + 1) * hb, last), 0)),
                  row_spec,
                  pl.BlockSpec((3, d), lambda i: (0, 0)),
                  pl.BlockSpec((d, d), lambda i: (0, 0)),
                  row_spec,
                  pl.BlockSpec((1, 1, d), lambda i: (i // tps, 0, 0))],
        out_specs=row_spec,
        out_shape=jax.ShapeDtypeStruct((t, d), F32),
        compiler_params=_params("arbitrary"),
        name="sc_conv_out_proj",
    )(cv, cv, cv, bg, conv_w, w_out, x2, gate)


def _mlp_kernel(x_ref, g_ref, sc_ref, sh_ref, gate_ref, w1_ref, w2_ref, fg_ref, o_ref, h_scr, *, final_norm):
    k = pl.program_id(1)

    def hidden_chunk(h):
        a = jnp.maximum(_dot(h, w1_ref[...]), 0.0)
        return _dot((a * a).astype(BF16), w2_ref[...])

    @pl.when(k == 0)
    def _():
        h = _modnorm(x_ref[...], g_ref[...], sc_ref[0], sh_ref[0]).astype(BF16)
        h_scr[...] = h
        o_ref[...] = hidden_chunk(h)

    last = pl.num_programs(1) - 1

    @pl.when((k != 0) & (k != last))
    def _():
        o_ref[...] += hidden_chunk(h_scr[...])

    @pl.when(k == last)
    def _():
        o_ref[...] += hidden_chunk(h_scr[...])
        slab = x_ref.shape[0] // MLP_EPILOGUE_SLABS
        for r in range(MLP_EPILOGUE_SLABS):
            rows = slice(r * slab, (r + 1) * slab)
            y = x_ref[rows, :] + gate_ref[0] * o_ref[rows, :]
            if final_norm:
                y = _rms(y, fg_ref[...])
            o_ref[rows, :] = y


def _mlp(x2, g, scale, shift, gate, w1, w2, final_g, seq, tm, tf, final_norm, to_cast=()):
    t, d = x2.shape
    ff = w1.shape[1]
    tps = seq // tm
    nk = ff // tf
    assert nk >= 2
    grid = (t // tm, nk)
    cast_in, cast_out, cast_shape = _cast_args(to_cast, grid[0] * grid[1], lambda i, k: i * nk + k)
    row_spec = pl.BlockSpec((tm, d), lambda i, k: (i, 0))
    vec_spec = pl.BlockSpec((1, d), lambda i, k: (0, 0))
    mod_spec = pl.BlockSpec((1, 1, d), lambda i, k: (i // tps, 0, 0))
    outs = pl.pallas_call(
        _with_casts(functools.partial(_mlp_kernel, final_norm=final_norm), 8, 1, len(to_cast)),
        grid=grid,
        in_specs=[row_spec, vec_spec, mod_spec, mod_spec, mod_spec,
                  pl.BlockSpec((d, tf), lambda i, k: (0, k)),
                  pl.BlockSpec((tf, d), lambda i, k: (k, 0)),
                  vec_spec] + cast_in,
        out_specs=[row_spec] + cast_out,
        out_shape=[jax.ShapeDtypeStruct((t, d), F32)] + cast_shape,
        scratch_shapes=[pltpu.VMEM((tm, d), BF16)],
        compiler_params=_params("arbitrary", "arbitrary"),
        name="mlp_final" if final_norm else "mlp",
    )(x2, g, scale, shift, gate, w1, w2, final_g, *[w for w, _ in to_cast])
    return outs[0], outs[1:]


def _m2_in_kernel(x_ref, xp_ref, xn_ref, g_ref, sc_ref, sh_ref, w_ref, wdt_ref, cw_ref, cb_ref,
                  o_ref, dt_ref, h_scr, *, tiles_per_seq, n_plain):
    j = pl.program_id(1)
    tm = x_ref.shape[0]

    @pl.when(j == 0)
    def _():
        h = _modnorm(x_ref[...], g_ref[...], sc_ref[0], sh_ref[0]).astype(BF16)
        h_scr[:tm, :] = h
        halo = jnp.concatenate([xp_ref[...], xn_ref[...]], axis=0)
        h_scr[tm:, :] = _modnorm(halo, g_ref[...], sc_ref[0], sh_ref[0]).astype(BF16)
        dt_ref[...] = _dot(h, wdt_ref[...])
        o_ref[...] = _dot(h, w_ref[...]).astype(BF16)

    @pl.when((j != 0) & (j < n_plain))
    def _():
        o_ref[...] = _dot(h_scr[:tm, :], w_ref[...]).astype(BF16)

    @pl.when(j >= n_plain)
    def _():
        i = pl.program_id(0) % tiles_per_seq
        proj = _dot(h_scr[...], w_ref[...])
        prev = jnp.where(i == 0, 0.0, proj[tm + SUBLANES - 1:tm + SUBLANES, :])
        nxt = jnp.where(i == tiles_per_seq - 1, 0.0, proj[tm + SUBLANES:tm + SUBLANES + 1, :])
        half = _conv3(proj[:tm, :], prev, nxt, 0.5 * cw_ref[...]) + 0.5 * cb_ref[...]
        o_ref[...] = _silu_of_half(half).astype(BF16)


def _m2_in(x2, g, scale, shift, w_in, conv_w, conv_b, n_plain_cols, seq, tm, tn):
    t, d = x2.shape
    n = n_plain_cols + conv_w.shape[1]
    ndt = w_in.shape[1] - n
    assert n_plain_cols % tn == 0 and n % tn == 0 and n % ndt == 0
    n_plain = n_plain_cols // tn
    tps = seq // tm
    hb = tm // SUBLANES
    last = t // SUBLANES - 1
    mod_spec = pl.BlockSpec((1, 1, d), lambda i, j: (i // tps, 0, 0))
    return pl.pallas_call(
        functools.partial(_m2_in_kernel, tiles_per_seq=tps, n_plain=n_plain),
        grid=(t // tm, n // tn),
        in_specs=[pl.BlockSpec((tm, d), lambda i, j: (i, 0)),
                  pl.BlockSpec((SUBLANES, d), lambda i, j: (jnp.maximum(i * hb - 1, 0), 0)),
                  pl.BlockSpec((SUBLANES, d), lambda i, j: (jnp.minimum((i + 1) * hb, last), 0)),
                  pl.BlockSpec((1, d), lambda i, j: (0, 0)),
                  mod_spec, mod_spec,
                  pl.BlockSpec((d, tn), lambda i, j: (0, j)),
                  pl.BlockSpec((d, ndt), lambda i, j: (0, n // ndt)),
                  pl.BlockSpec((3, tn), lambda i, j: (0, jnp.maximum(j - n_plain, 0))),
                  pl.BlockSpec((1, tn), lambda i, j: (0, jnp.maximum(j - n_plain, 0)))],
        out_specs=[pl.BlockSpec((tm, tn), lambda i, j: (i, j)),
                   pl.BlockSpec((tm, ndt), lambda i, j: (i, 0))],
        out_shape=[jax.ShapeDtypeStruct((t, n), BF16), jax.ShapeDtypeStruct((t, ndt), F32)],
        scratch_shapes=[pltpu.VMEM((tm + 2 * SUBLANES, d), BF16)],
        compiler_params=_params("arbitrary", "arbitrary"),
        name="m2_in_proj",
    )(x2, x2, x2, g, scale, shift, w_in, w_in, conv_w, conv_b)


N_PIECES = 3
GROUP_LANES = 128


def _piece_base(heads_per_group):
    w = N_PIECES * heads_per_group
    return {"cum_f": 0, "cum_b": w, "ecum_f": 2 * w, "ecum_b": 3 * w, "wend_f": 4 * w}


def _perm_matrices(n_heads, heads_per_group):
    rr = heads_per_group
    base = _piece_base(rr)
    nat = 2 * n_heads
    perm_a = np.zeros((3 * N_PIECES * nat, N_GROUPS * GROUP_LANES), np.float32)
    perm_w = np.zeros((N_PIECES * nat, N_GROUPS * GROUP_LANES), np.float32)
    for piece in range(N_PIECES):
        for d in range(2):
            for h in range(n_heads):
                g, r = divmod(h, rr)
                lane = piece * nat + d * n_heads + h
                col = g * GROUP_LANES + piece * rr + r
                perm_a[0 * N_PIECES * nat + lane, col + (base["cum_f"] if d == 0 else base["cum_b"])] = 1
                perm_a[1 * N_PIECES * nat + lane, col + (base["ecum_f"] if d == 0 else base["ecum_b"])] = 1
                if d == 0:
                    perm_a[2 * N_PIECES * nat + lane, col + base["wend_f"]] = 1
                else:
                    perm_w[lane, col] = 1
    return jnp.asarray(perm_a, BF16), jnp.asarray(perm_w, BF16)


def _select_matrices(heads_per_group, head_dim, q):
    rr = heads_per_group
    base = _piece_base(rr)
    gw = rr * head_dim
    sel_c = np.zeros((GROUP_LANES, 2 * rr * q), np.float32)
    sel_e = np.zeros((GROUP_LANES, 3 * gw), np.float32)
    sel_w = np.zeros((GROUP_LANES, gw), np.float32)
    for piece in range(N_PIECES):
        for r in range(rr):
            k = piece * rr + r
            sel_c[base["cum_f"] + k, r * q:(r + 1) * q] = 1
            sel_c[base["cum_b"] + k, (rr + r) * q:(rr + r + 1) * q] = 1
            sel_e[base["ecum_f"] + k, r * head_dim:(r + 1) * head_dim] = 1
            sel_e[base["ecum_b"] + k, gw + r * head_dim:gw + (r + 1) * head_dim] = 1
            sel_e[base["wend_f"] + k, 2 * gw + r * head_dim:2 * gw + (r + 1) * head_dim] = 1
            sel_w[k, r * head_dim:(r + 1) * head_dim] = 1
    return jnp.asarray(sel_c, BF16), jnp.asarray(sel_e, BF16), jnp.asarray(sel_w, BF16)


def _bf16_pieces(v):
    hi = v.astype(BF16)
    rem = v - hi.astype(F32)
    mid = rem.astype(BF16)
    lo = (rem - mid.astype(F32)).astype(BF16)
    return [hi, mid, lo]


def _dt_kernel(raw_ref, bias_ref, alog_ref, perm_a_ref, perm_w_ref,
               a_ref, w_ref, rowt_ref, diagt_ref, elast_ref, *, n_heads):
    q = CHUNK
    w = raw_ref.shape[1]
    r = lax.broadcasted_iota(jnp.int32, (q, q), 0)
    c = lax.broadcasted_iota(jnp.int32, (q, q), 1)
    lower = jnp.where(c <= r, 1.0, 0.0).astype(BF16)
    upper = jnp.where(c >= r, 1.0, 0.0).astype(BF16)
    fwd = lax.broadcasted_iota(jnp.int32, (q, w), 1) < n_heads
    neg_a2 = jnp.exp(alog_ref[...]) * LOG2_E
    lhs_a, lhs_w = [], []
    for k in range(raw_ref.shape[0] // q):
        rows = slice(k * q, (k + 1) * q)
        v = raw_ref[rows, :] + bias_ref[...]
        dt = jnp.maximum(v, 0.0) + jnp.log1p(jnp.exp(-jnp.abs(v)))
        a2 = -(dt * neg_a2)
        pieces = jnp.concatenate(_bf16_pieces(a2), axis=1)
        pre = _dot(lower, pieces)
        suf = _dot(upper, pieces)
        cum = jnp.where(fwd, pre[:, :w] + pre[:, w:2 * w] + pre[:, 2 * w:],
                        suf[:, :w] + suf[:, w:2 * w] + suf[:, 2 * w:])
        last = jnp.where(fwd[0:1, :], cum[q - 1:q, :], cum[0:1, :])
        wend = _bf16_pieces(jnp.exp2(last - cum) * dt)
        lhs_a.append(jnp.concatenate(_bf16_pieces(cum) + _bf16_pieces(jnp.exp2(cum)) + wend, axis=1))
        lhs_w.append(jnp.concatenate(wend, axis=1))
        rowt_ref[0, :, rows] = (cum - jnp.log2(dt)).T
        diagt_ref[0, :, rows] = jnp.log2(dt + pltpu.roll(dt, n_heads, 1)).T
        elast_ref[k] = jnp.exp2(last)
    a_ref[...] = _dot(jnp.concatenate(lhs_a, axis=0), perm_a_ref[...]).astype(BF16)
    w_ref[...] = _dot(jnp.concatenate(lhs_w, axis=0), perm_w_ref[...]).astype(BF16)


def _dt_prep(dt_raw, dt_bias, a_log, bsz, seq, heads_per_group):
    t, w = dt_raw.shape
    q = CHUNK
    nc = seq // q
    perm_a, perm_w = _perm_matrices(w // 2, heads_per_group)
    gl = N_GROUPS * GROUP_LANES
    per_step = DT_CHUNKS_PER_STEP
    assert nc % per_step == 0
    rows = per_step * q
    nblk = nc // per_step
    col_spec = pl.BlockSpec((rows, w), lambda i: (i, 0))
    vec_spec = pl.BlockSpec((1, w), lambda i: (0, 0))
    grp_spec = pl.BlockSpec((rows, gl), lambda i: (i, 0))
    return pl.pallas_call(
        functools.partial(_dt_kernel, n_heads=w // 2),
        grid=(t // rows,),
        in_specs=[col_spec, vec_spec, vec_spec,
                  pl.BlockSpec(perm_a.shape, lambda i: (0, 0)),
                  pl.BlockSpec(perm_w.shape, lambda i: (0, 0))],
        out_specs=[grp_spec, grp_spec,
                   pl.BlockSpec((1, w, rows), lambda i: (i // nblk, 0, i % nblk)),
                   pl.BlockSpec((1, w, rows), lambda i: (i // nblk, 0, i % nblk)),
                   pl.BlockSpec((per_step, 1, w), lambda i: (i, 0, 0))],
        out_shape=[jax.ShapeDtypeStruct((t, gl), BF16), jax.ShapeDtypeStruct((t, gl), BF16),
                   jax.ShapeDtypeStruct((bsz, w, seq), F32), jax.ShapeDtypeStruct((bsz, w, seq), F32),
                   jax.ShapeDtypeStruct((t // q, 1, w), F32)],
        compiler_params=_params("arbitrary"),
        name="m2_dt_prep",
    )(dt_raw, dt_bias, a_log, perm_a, perm_w)


def _ssd_backward_block(blk, x_ref, b_ref, w_ref, selw_ref, el_ref, stb, sb_scr, *, q, per_step):
    ks = list(reversed(range(per_step)))
    rows = [slice(k * q, (k + 1) * q) for k in range(per_step)]
    wexp = {k: _dot(w_ref[rows[k], :], selw_ref[...]) for k in ks}
    upd = {}
    for k in ks:
        xw = (x_ref[rows[k], :].astype(F32) * wexp[k]).astype(BF16)
        bt = b_ref[rows[k], :].astype(F32).T.astype(BF16)
        upd[k] = _dot(bt, xw)
    st = stb[...]
    for k in ks:
        sb_scr[blk * per_step + k] = st.astype(BF16)
        st = st * el_ref[0, k, 1:2, :] + upd[k]
    stb[...] = st


def _ssd_forward_block(blk, x_ref, b_ref, c_ref, z_ref, a_ref, selc_ref, sele_ref, rtf_ref, rtb_ref, dg_ref,
                       el_ref, dsk_ref, ng_ref, u_ref, stf, sb_scr, *, q, per_step, heads_per_group, head_dim):
    rr = heads_per_group
    gw = rr * head_dim
    ks = list(range(per_step))
    rows = [slice(k * q, (k + 1) * q) for k in ks]
    t_idx = lax.broadcasted_iota(jnp.int32, (q, q), 0)
    s_idx = lax.broadcasted_iota(jnp.int32, (q, q), 1)
    below = t_idx > s_idx
    above = t_idx < s_idx
    lane_head = lax.broadcasted_iota(jnp.int32, (q, 256), 1) // head_dim
    n_pairs = rr // 2
    heads_per_tile = 256 // head_dim

    scores, col_f, col_b, exp_f, exp_b, exp_w = {}, {}, {}, {}, {}, {}
    for k in ks:
        a = a_ref[rows[k], :]
        scores[k] = lax.dot_general(c_ref[rows[k], :], b_ref[rows[k], :], (((1,), (1,)), ((), ())),
                                    preferred_element_type=F32)
        for pr in range(n_pairs):
            col_f[k, pr] = _dot(a, selc_ref[:, 2 * pr * q:(2 * pr + 2) * q])
            col_b[k, pr] = _dot(a, selc_ref[:, (rr + 2 * pr) * q:(rr + 2 * pr + 2) * q])
        exp_f[k] = _dot(a, sele_ref[:, 0:gw])
        exp_b[k] = _dot(a, sele_ref[:, gw:2 * gw])
        exp_w[k] = _dot(a, sele_ref[:, 2 * gw:3 * gw])

    y, upd = {}, {}
    for k in ks:
        x = x_ref[rows[k], :]
        rtf, rtb, dg = rtf_ref[0, :, rows[k]], rtb_ref[0, :, rows[k]], dg_ref[0, :, rows[k]]
        y_parts = []
        for tile in range(gw // 256):
            xt = x[:, tile * 256:(tile + 1) * 256]
            acc = None
            for pair in range(heads_per_tile // 2):
                pr = tile * (heads_per_tile // 2) + pair
                ms, xs = [], []
                for j in range(2):
                    h = 2 * pr + j
                    cols = slice(j * q, (j + 1) * q)
                    seg = jnp.where(below, col_f[k, pr][:, cols] - rtf[h:h + 1, :],
                                    jnp.where(above, col_b[k, pr][:, cols] - rtb[h:h + 1, :], dg[h:h + 1, :]))
                    ms.append((scores[k] * jnp.exp2(seg)).astype(BF16))
                    xs.append(jnp.where(lane_head == 2 * pair + j, xt, jnp.zeros_like(xt)))
                part = _dot(jnp.concatenate(ms, axis=1), jnp.concatenate(xs, axis=0))
                acc = part if acc is None else acc + part
            y_parts.append(acc)
        xf32 = x.astype(F32)
        y[k] = (jnp.concatenate(y_parts, axis=1) + dsk_ref[...] * xf32
                + exp_b[k] * _dot(c_ref[rows[k], :], sb_scr[blk * per_step + k]))
        xw = (xf32 * exp_w[k]).astype(BF16)
        bt = b_ref[rows[k], :].astype(F32).T.astype(BF16)
        upd[k] = _dot(bt, xw)

    st = stf[...]
    sts = {}
    for k in ks:
        sts[k] = st.astype(BF16)
        st = st * el_ref[0, k, 0:1, :] + upd[k]
    stf[...] = st
    for k in ks:
        yk = y[k] + exp_f[k] * _dot(c_ref[rows[k], :], sts[k])
        u = yk * _silu(z_ref[rows[k], :].astype(F32))
        u_ref[rows[k], :] = _rms(u, ng_ref[...]).astype(BF16)


def _ssd_kernel(x_ref, b_ref, c_ref, z_ref, a_ref, w_ref, selc_ref, sele_ref, selw_ref,
                rtf_ref, rtb_ref, dg_ref, el_ref, dsk_ref, ng_ref,
                u_ref, stf, stb, sb_scr, *, heads_per_group, head_dim, per_step):
    q = x_ref.shape[0] // per_step
    phase = pl.program_id(2)
    step = pl.program_id(3)
    nblk = pl.num_programs(3)

    @pl.when(phase == 0)
    def _():
        @pl.when(step == 0)
        def _():
            stb[...] = jnp.zeros_like(stb)

        _ssd_backward_block(nblk - 1 - step, x_ref, b_ref, w_ref, selw_ref, el_ref, stb, sb_scr,
                            q=q, per_step=per_step)

    @pl.when(phase == 1)
    def _():
        @pl.when(step == 0)
        def _():
            stf[...] = jnp.zeros_like(stf)

        _ssd_forward_block(step, x_ref, b_ref, c_ref, z_ref, a_ref, selc_ref, sele_ref, rtf_ref, rtb_ref, dg_ref,
                           el_ref, dsk_ref, ng_ref, u_ref, stf, sb_scr, q=q, per_step=per_step,
                           heads_per_group=heads_per_group, head_dim=head_dim)


def _ssd(zxbc, a_cols, w_cols, row_t, diag_t, elast, dskip, norm_g, bsz, seq, d_inner, d_state, n_heads,
         to_cast):
    t = zxbc.shape[0]
    q = CHUNK
    nc = seq // q
    per_step = SSD_CHUNKS_PER_STEP
    assert nc % per_step == 0
    nblk = nc // per_step
    rows = per_step * q
    g = N_GROUPS
    rr = n_heads // g
    p = d_inner // n_heads
    gw = rr * p
    assert p == 64 and d_state == 128 and q == 128 and gw % 256 == 0 and rr % 2 == 0
    assert 5 * N_PIECES * rr <= GROUP_LANES and d_inner // g == gw
    xoff = d_inner // gw
    boff = 2 * d_inner // d_state
    sel_c, sel_e, sel_w = _select_matrices(rr, p, q)
    elast = elast.reshape(bsz * nblk, per_step, 2, d_inner)

    def both(b, gi, ph, s):
        return b * nblk + ph * s + (1 - ph) * (nblk - 1 - s)

    def fwd(b, gi, ph, s):
        return b * nblk + ph * s

    def bwd(b, gi, ph, s):
        return b * nblk + (1 - ph) * (nblk - 1 - s)

    const = lambda arr: pl.BlockSpec(arr.shape, lambda b, gi, ph, s: (0, 0))
    grid = (bsz, g, 2, nblk)
    cast_in, cast_out, cast_shape = _cast_args(to_cast, bsz * g * 2 * nblk,
                                               lambda b, gi, ph, s: ((b * g + gi) * 2 + ph) * nblk + s)
    outs = pl.pallas_call(
        _with_casts(functools.partial(_ssd_kernel, heads_per_group=rr, head_dim=p, per_step=per_step),
                    15, 1, len(to_cast)),
        grid=grid,
        in_specs=[pl.BlockSpec((rows, gw), lambda b, gi, ph, s: (both(b, gi, ph, s), xoff + gi)),
                  pl.BlockSpec((rows, d_state), lambda b, gi, ph, s: (both(b, gi, ph, s), boff + gi)),
                  pl.BlockSpec((rows, d_state), lambda b, gi, ph, s: (fwd(b, gi, ph, s), boff + g + gi)),
                  pl.BlockSpec((rows, gw), lambda b, gi, ph, s: (fwd(b, gi, ph, s), gi)),
                  pl.BlockSpec((rows, GROUP_LANES), lambda b, gi, ph, s: (fwd(b, gi, ph, s), gi)),
                  pl.BlockSpec((rows, GROUP_LANES), lambda b, gi, ph, s: (bwd(b, gi, ph, s), gi)),
                  const(sel_c), const(sel_e), const(sel_w),
                  pl.BlockSpec((1, rr, rows), lambda b, gi, ph, s: (b, gi, ph * s)),
                  pl.BlockSpec((1, rr, rows), lambda b, gi, ph, s: (b, g + gi, ph * s)),
                  pl.BlockSpec((1, rr, rows), lambda b, gi, ph, s: (b, gi, ph * s)),
                  pl.BlockSpec((1, per_step, 2, gw), lambda b, gi, ph, s: (both(b, gi, ph, s), 0, 0, gi)),
                  pl.BlockSpec((1, gw), lambda b, gi, ph, s: (0, gi)),
                  pl.BlockSpec((1, gw), lambda b, gi, ph, s: (0, gi))] + cast_in,
        out_specs=[pl.BlockSpec((rows, gw), lambda b, gi, ph, s: (fwd(b, gi, ph, s), gi))] + cast_out,
        out_shape=[jax.ShapeDtypeStruct((t, d_inner), BF16)] + cast_shape,
        scratch_shapes=[pltpu.VMEM((d_state, gw), F32), pltpu.VMEM((d_state, gw), F32),
                        pltpu.VMEM((nc, d_state, gw), BF16)],
        compiler_params=_params("arbitrary", "arbitrary", "arbitrary", "arbitrary"),
        name="m2_ssd",
    )(zxbc, zxbc, zxbc, zxbc, a_cols, w_cols, sel_c, sel_e, sel_w, row_t, row_t, diag_t, elast, dskip, norm_g,
      *[w for w, _ in to_cast])
    return outs[0], outs[1:]


def _m2_out_kernel(u_ref, w_ref, x_ref, gate_ref, o_ref):
    o_ref[...] = x_ref[...] + gate_ref[0] * _dot(u_ref[...], w_ref[...])


def _m2_out(u, w_out, x2, gate, seq, tm, tn):
    t, d = x2.shape
    di = u.shape[1]
    tps = seq // tm
    return pl.pallas_call(
        _m2_out_kernel,
        grid=(t // tm, d // tn),
        in_specs=[pl.BlockSpec((tm, di), lambda i, j: (i, 0)),
                  pl.BlockSpec((di, tn), lambda i, j: (0, j)),
                  pl.BlockSpec((tm, tn), lambda i, j: (i, j)),
                  pl.BlockSpec((1, 1, tn), lambda i, j: (i // tps, 0, j))],
        out_specs=pl.BlockSpec((tm, tn), lambda i, j: (i, j)),
        out_shape=jax.ShapeDtypeStruct((t, d), F32),
        compiler_params=_params("arbitrary", "arbitrary"),
        name="m2_out_proj",
    )(u, w_out, x2, gate)


def _row_tile(seq, want):
    tm = min(want, seq)
    assert seq % tm == 0 and tm % HALO_ROWS == 0
    return tm


def _tiles(seq):
    return dict(tm=_row_tile(seq, 1024), tm_conv=_row_tile(seq, 512), tn=512, tn_wide=1024, tf=1024)


def kernel(x, c, ada_w, ada_b, norm_g, final_g, sc_in_w, sc_conv_w, sc_out_w, m2_in_w, m2_conv_w, m2_conv_b,
           m2_dt_bias, m2_a_log, m2_d, m2_norm_g, m2_out_w, mlp_w1, mlp_w2):
    bsz, seq, d = x.shape
    t = bsz * seq
    d_inner = m2_norm_g.shape[1]
    n_heads = m2_d.shape[1]
    conv_dim = m2_conv_w.shape[1]
    d_state = (conv_dim - d_inner) // (2 * N_GROUPS)
    head_dim = d_inner // n_heads
    assert seq % CHUNK == 0
    ts = _tiles(seq)
    tm, tn, tf = ts["tm"], ts["tn"], ts["tf"]

    mods = _modulation(c, ada_w, ada_b)
    x2 = x.reshape(t, d)
    vec = lambda v: v.reshape(1, -1)

    shift, scale, gate = mods[0]
    bg, cv, (sc_out_bf, w1_bf, w2_bf) = _sc_in(x2, vec(norm_g[0, 0]), scale, shift, sc_in_w[0].astype(BF16),
                                               seq, tm, tn, [(sc_out_w, 0), (mlp_w1, 0), (mlp_w2, 0)])
    x2 = _sc_out(cv, bg, sc_conv_w[0].T, sc_out_bf, x2, gate, seq, ts["tm_conv"])
    shift, scale, gate = mods[1]
    x2, (m2_in_bf,) = _mlp(x2, vec(norm_g[0, 1]), scale, shift, gate, w1_bf, w2_bf,
                           vec(final_g), seq, tm, tf, final_norm=False, to_cast=[(m2_in_w, 0)])

    shift, scale, gate = mods[2]
    zxbc, dt_raw = _m2_in(x2, vec(norm_g[1, 0]), scale, shift, m2_in_bf, m2_conv_w[0].T, vec(m2_conv_b[0]),
                          d_inner, seq, tm, ts["tn_wide"])
    rr = n_heads // N_GROUPS
    a_cols, w_cols, row_t, diag_t, elast = _dt_prep(dt_raw, vec(m2_dt_bias[0]), vec(m2_a_log[0]), bsz, seq, rr)
    elast = jnp.repeat(elast.reshape(-1, 2, n_heads), head_dim, axis=2)
    dskip = jnp.repeat(m2_d[0], head_dim).reshape(1, d_inner)
    u, (m2_out_bf, w1_bf, w2_bf) = _ssd(zxbc, a_cols, w_cols, row_t, diag_t, elast, dskip, vec(m2_norm_g[0]),
                                        bsz, seq, d_inner, d_state, n_heads,
                                        [(m2_out_w, 0), (mlp_w1, 1), (mlp_w2, 1)])
    x2 = _m2_out(u, m2_out_bf, x2, gate, seq, tm, ts["tn_wide"])
    shift, scale, gate = mods[3]
    x2, _ = _mlp(x2, vec(norm_g[1, 1]), scale, shift, gate, w1_bf, w2_bf,
                 vec(final_g), seq, tm, tf, final_norm=True)
    return x2.reshape(bsz, seq, d)
```

```python
import functools

import numpy as np
import jax
import jax.numpy as jnp
from jax import lax
from jax.experimental import pallas as pl
from jax.experimental.pallas import tpu as pltpu

N_GROUPS = 8
CHUNK = 128
MLP_EPILOGUE_SLABS = 4
DT_CHUNKS_PER_STEP = 8
SSD_CHUNKS_PER_STEP = 16
EPS = 1e-6
LOG2_E = 1.4426950408889634
SUBLANES = 8
LANES = 128
HALO_ROWS = 16
VMEM_LIMIT_BYTES = 62 * 1024 * 1024

F32 = jnp.float32
BF16 = jnp.bfloat16


def _params(*sem):
    return pltpu.CompilerParams(dimension_semantics=sem, vmem_limit_bytes=VMEM_LIMIT_BYTES)


def _dot(a, b):
    return jnp.dot(a, b, preferred_element_type=F32)


def _silu_of_half(half):
    return half * jnp.tanh(half) + half


def _silu(x):
    return _silu_of_half(0.5 * x)


def _rms(x, g):
    ms = jnp.mean(x * x, axis=-1, keepdims=True)
    return (x * lax.rsqrt(ms + EPS)) * g


def _modnorm(x, g, scale, shift):
    return _rms(x, g) * (1.0 + scale) + shift


def _conv3(main, prev_row, next_row, w):
    tm, width = main.shape
    core = w[0:1, :] * pltpu.roll(main, 1, 0) + w[1:2, :] * main + w[2:3, :] * pltpu.roll(main, tm - 1, 0)
    row = lax.broadcasted_iota(jnp.int32, (SUBLANES, width), 0)
    first = jnp.where(row == 0, w[0:1, :] * (prev_row - main[tm - 1:tm, :]), 0.0)
    last = jnp.where(row == SUBLANES - 1, w[2:3, :] * (next_row - main[0:1, :]), 0.0)
    return jnp.concatenate([core[:SUBLANES] + first, core[SUBLANES:tm - SUBLANES], core[tm - SUBLANES:] + last],
                           axis=0)


def _halo_rows(prev_ref, next_ref, tiles_per_seq):
    i = pl.program_id(0) % tiles_per_seq
    prev = prev_ref[...].astype(F32)[HALO_ROWS - 1:HALO_ROWS, :]
    nxt = next_ref[...].astype(F32)[0:1, :]
    prev = jnp.where(i == 0, 0.0, prev)
    nxt = jnp.where(i == tiles_per_seq - 1, 0.0, nxt)
    return prev, nxt


def _with_casts(body, n_in, n_out, n_cast):
    def kern(*refs):
        ins, rest = refs[:n_in], refs[n_in:]
        cast_in, rest = rest[:n_cast], rest[n_cast:]
        outs, rest = rest[:n_out], rest[n_out:]
        cast_out, scratch = rest[:n_cast], rest[n_cast:]
        for src, dst in zip(cast_in, cast_out):
            if len(dst.shape) == 3:
                width = dst.shape[2]
                for jb in range(dst.shape[0]):
                    dst[jb] = src[:, jb * width:(jb + 1) * width].astype(BF16)
            else:
                dst[...] = src[...].astype(BF16)
        body(*ins, *outs, *scratch)
    return kern


def _cast_specs(stacked, layer, col_block, n_steps, lin):
    _, r, c = stacked.shape
    nb = n_steps
    while r % nb or (r // nb) % HALO_ROWS:
        nb //= 2
    blk = lambda *idx: lin(*idx) * nb // n_steps
    src = pl.BlockSpec((None, r // nb, c), lambda *idx: (layer, blk(*idx), 0))
    if col_block is None:
        return src, pl.BlockSpec((r // nb, c), lambda *idx: (blk(*idx), 0)), jax.ShapeDtypeStruct((r, c), BF16)
    ncb = c // col_block
    dst = pl.BlockSpec((ncb, r // nb, col_block), lambda *idx: (0, blk(*idx), 0))
    return src, dst, jax.ShapeDtypeStruct((ncb, r, col_block), BF16)


def _cast_args(to_cast, n_steps, lin):
    specs = [_cast_specs(w, layer, col_block, n_steps, lin) for w, layer, col_block in to_cast]
    return [s[0] for s in specs], [s[1] for s in specs], [s[2] for s in specs]


def _mod_kernel(c_ref, w_ref, b_ref, o_ref, cond_scr):
    @pl.when((pl.program_id(0) == 0) & (pl.program_id(1) == 0))
    def _():
        cond_scr[...] = _silu(c_ref[...])

    for slab in range(w_ref.shape[2] // LANES):
        cols = slice(slab * LANES, (slab + 1) * LANES)
        w = w_ref[0, :, cols]
        for b in range(c_ref.shape[0]):
            o_ref[0, b:b + 1, cols] = jnp.sum(w * cond_scr[b], axis=0, keepdims=True) + b_ref[0, :, cols]


def _modulation(c, ada_w, ada_b):
    bsz, d = c.shape
    n_mod = ada_w.shape[0] * ada_w.shape[1]
    w = ada_w.reshape(n_mod, d, 3 * d)
    b = ada_b.reshape(n_mod, 1, 3 * d)
    c_rep = jnp.broadcast_to(c[:, :, None], (bsz, d, LANES))
    tn = 1536
    out = pl.pallas_call(
        _mod_kernel,
        grid=(n_mod, 3 * d // tn),
        in_specs=[pl.BlockSpec((bsz, d, LANES), lambda m, j: (0, 0, 0)),
                  pl.BlockSpec((1, d, tn), lambda m, j: (m, 0, j)),
                  pl.BlockSpec((1, 1, tn), lambda m, j: (m, 0, j))],
        out_specs=pl.BlockSpec((1, bsz, tn), lambda m, j: (m, 0, j)),
        out_shape=jax.ShapeDtypeStruct((n_mod, bsz, 3 * d), F32),
        scratch_shapes=[pltpu.VMEM((bsz, d, LANES), F32)],
        compiler_params=_params("arbitrary", "arbitrary"),
        name="modulation",
    )(c_rep, w, b)
    out = out.reshape(n_mod, bsz, 3, 1, d)
    return [(out[m, :, 0], out[m, :, 1], out[m, :, 2]) for m in range(n_mod)]


def _sc_in_kernel(x_ref, g_ref, sc_ref, sh_ref, wb_ref, wc_ref, wv_ref, b_out, cv_out, h_scr):
    def project(h):
        b_out[...] = _dot(h, wb_ref[...]).astype(BF16)
        cv_out[...] = (_dot(h, wc_ref[...]) * _dot(h, wv_ref[...])).astype(BF16)

    @pl.when(pl.program_id(1) == 0)
    def _():
        h = _modnorm(x_ref[...], g_ref[...], sc_ref[0], sh_ref[0]).astype(BF16)
        h_scr[...] = h
        project(h)

    @pl.when(pl.program_id(1) != 0)
    def _():
        project(h_scr[...])


def _sc_in(x2, g, scale, shift, w_in, seq, tm, tn, to_cast):
    t, d = x2.shape
    tps = seq // tm
    nb = d // tn
    grid = (t // tm, nb)
    cast_in, cast_out, cast_shape = _cast_args(to_cast, grid[0] * grid[1], lambda i, j: i * nb + j)
    row_spec = pl.BlockSpec((tm, d), lambda i, j: (i, 0))
    vec_spec = pl.BlockSpec((1, d), lambda i, j: (0, 0))
    mod_spec = pl.BlockSpec((1, 1, d), lambda i, j: (i // tps, 0, 0))
    out_spec = pl.BlockSpec((tm, tn), lambda i, j: (i, j))
    outs = pl.pallas_call(
        _with_casts(_sc_in_kernel, 7, 2, len(to_cast)),
        grid=grid,
        in_specs=[row_spec, vec_spec, mod_spec, mod_spec,
                  pl.BlockSpec((d, tn), lambda i, j: (0, j)),
                  pl.BlockSpec((d, tn), lambda i, j: (0, j + nb)),
                  pl.BlockSpec((d, tn), lambda i, j: (0, j + 2 * nb))] + cast_in,
        out_specs=[out_spec, out_spec] + cast_out,
        out_shape=[jax.ShapeDtypeStruct((t, d), BF16), jax.ShapeDtypeStruct((t, d), BF16)] + cast_shape,
        scratch_shapes=[pltpu.VMEM((tm, d), BF16)],
        compiler_params=_params("arbitrary", "arbitrary"),
        name="sc_in_proj",
    )(x2, g, scale, shift, w_in, w_in, w_in, *[spec[0] for spec in to_cast])
    return outs[0], outs[1], outs[2:]


def _sc_out_kernel(cv_ref, cvp_ref, cvn_ref, bg_ref, cw_ref, w_ref, x_ref, gate_ref, o_ref, *, tiles_per_seq):
    prev, nxt = _halo_rows(cvp_ref, cvn_ref, tiles_per_seq)
    u = _conv3(cv_ref[...].astype(F32), prev, nxt, cw_ref[...])
    u = (u * bg_ref[...].astype(F32)).astype(BF16)
    o_ref[...] = x_ref[...] + gate_ref[0] * _dot(u, w_ref[...])


def _sc_out(cv, bg, conv_w, w_out, x2, gate, seq, tm):
    t, d = x2.shape
    tps = seq // tm
    hb = tm // HALO_ROWS
    last = t // HALO_ROWS - 1
    row_spec = pl.BlockSpec((tm, d), lambda i: (i, 0))
    return pl.pallas_call(
        functools.partial(_sc_out_kernel, tiles_per_seq=tps),
        grid=(t // tm,),
        in_specs=[row_spec,
                  pl.BlockSpec((HALO_ROWS, d), lambda i: (jnp.maximum(i * hb - 1, 0), 0)),
                  pl.BlockSpec((HALO_ROWS, d), lambda i: (jnp.minimum((i + 1) * hb, last), 0)),
                  row_spec,
                  pl.BlockSpec((3, d), lambda i: (0, 0)),
                  pl.BlockSpec((d, d), lambda i: (0, 0)),
                  row_spec,
                  pl.BlockSpec((1, 1, d), lambda i: (i // tps, 0, 0))],
        out_specs=row_spec,
        out_shape=jax.ShapeDtypeStruct((t, d), F32),
        compiler_params=_params("arbitrary"),
        name="sc_conv_out_proj",
    )(cv, cv, cv, bg, conv_w, w_out, x2, gate)


def _mlp_kernel(x_ref, g_ref, sc_ref, sh_ref, gate_ref, w1_ref, w2_ref, fg_ref, o_ref, h_scr, *, final_norm):
    k = pl.program_id(1)

    def hidden_chunk(h):
        a = jnp.maximum(_dot(h, w1_ref[...]), 0.0)
        return _dot((a * a).astype(BF16), w2_ref[...])

    @pl.when(k == 0)
    def _():
        h = _modnorm(x_ref[...], g_ref[...], sc_ref[0], sh_ref[0]).astype(BF16)
        h_scr[...] = h
        o_ref[...] = hidden_chunk(h)

    last = pl.num_programs(1) - 1

    @pl.when((k != 0) & (k != last))
    def _():
        o_ref[...] += hidden_chunk(h_scr[...])

    @pl.when(k == last)
    def _():
        o_ref[...] += hidden_chunk(h_scr[...])
        slab = x_ref.shape[0] // MLP_EPILOGUE_SLABS
        for r in range(MLP_EPILOGUE_SLABS):
            rows = slice(r * slab, (r + 1) * slab)
            y = x_ref[rows, :] + gate_ref[0] * o_ref[rows, :]
            if final_norm:
                y = _rms(y, fg_ref[...])
            o_ref[rows, :] = y


def _mlp(x2, g, scale, shift, gate, w1, w2, final_g, seq, tm, final_norm, to_cast=()):
    t, d = x2.shape
    nk, _, tf = w1.shape
    tps = seq // tm
    assert nk >= 2
    grid = (t // tm, nk)
    cast_in, cast_out, cast_shape = _cast_args(to_cast, grid[0] * grid[1], lambda i, k: i * nk + k)
    row_spec = pl.BlockSpec((tm, d), lambda i, k: (i, 0))
    vec_spec = pl.BlockSpec((1, d), lambda i, k: (0, 0))
    mod_spec = pl.BlockSpec((1, 1, d), lambda i, k: (i // tps, 0, 0))
    outs = pl.pallas_call(
        _with_casts(functools.partial(_mlp_kernel, final_norm=final_norm), 8, 1, len(to_cast)),
        grid=grid,
        in_specs=[row_spec, vec_spec, mod_spec, mod_spec, mod_spec,
                  pl.BlockSpec((None, d, tf), lambda i, k: (k, 0, 0)),
                  pl.BlockSpec((tf, d), lambda i, k: (k, 0)),
                  vec_spec] + cast_in,
        out_specs=[row_spec] + cast_out,
        out_shape=[jax.ShapeDtypeStruct((t, d), F32)] + cast_shape,
        scratch_shapes=[pltpu.VMEM((tm, d), BF16)],
        compiler_params=_params("arbitrary", "arbitrary"),
        name="mlp_final" if final_norm else "mlp",
    )(x2, g, scale, shift, gate, w1, w2, final_g, *[spec[0] for spec in to_cast])
    return outs[0], outs[1:]


def _m2_in_kernel(x_ref, xp_ref, xn_ref, g_ref, sc_ref, sh_ref, w_ref, wdt_ref, cw_ref, cb_ref,
                  o_ref, dt_ref, h_scr, *, tiles_per_seq, n_plain):
    j = pl.program_id(1)
    tm = x_ref.shape[0]

    @pl.when(j == 0)
    def _():
        h = _modnorm(x_ref[...], g_ref[...], sc_ref[0], sh_ref[0]).astype(BF16)
        h_scr[:tm, :] = h
        halo = jnp.concatenate([xp_ref[...], xn_ref[...]], axis=0)
        h_scr[tm:, :] = _modnorm(halo, g_ref[...], sc_ref[0], sh_ref[0]).astype(BF16)
        dt_ref[...] = _dot(h, wdt_ref[...])
        o_ref[...] = _dot(h, w_ref[...]).astype(BF16)

    @pl.when((j != 0) & (j < n_plain))
    def _():
        o_ref[...] = _dot(h_scr[:tm, :], w_ref[...]).astype(BF16)

    @pl.when(j >= n_plain)
    def _():
        i = pl.program_id(0) % tiles_per_seq
        proj = _dot(h_scr[...], w_ref[...])
        prev = jnp.where(i == 0, 0.0, proj[tm + SUBLANES - 1:tm + SUBLANES, :])
        nxt = jnp.where(i == tiles_per_seq - 1, 0.0, proj[tm + SUBLANES:tm + SUBLANES + 1, :])
        half = _conv3(proj[:tm, :], prev, nxt, 0.5 * cw_ref[...]) + 0.5 * cb_ref[...]
        o_ref[...] = _silu_of_half(half).astype(BF16)


def _m2_in(x2, g, scale, shift, w_in, conv_w, conv_b, n_plain_cols, seq, tm, tn):
    t, d = x2.shape
    n = n_plain_cols + conv_w.shape[1]
    ndt = w_in.shape[1] - n
    assert n_plain_cols % tn == 0 and n % tn == 0 and n % ndt == 0
    n_plain = n_plain_cols // tn
    tps = seq // tm
    hb = tm // SUBLANES
    last = t // SUBLANES - 1
    mod_spec = pl.BlockSpec((1, 1, d), lambda i, j: (i // tps, 0, 0))
    return pl.pallas_call(
        functools.partial(_m2_in_kernel, tiles_per_seq=tps, n_plain=n_plain),
        grid=(t // tm, n // tn),
        in_specs=[pl.BlockSpec((tm, d), lambda i, j: (i, 0)),
                  pl.BlockSpec((SUBLANES, d), lambda i, j: (jnp.maximum(i * hb - 1, 0), 0)),
                  pl.BlockSpec((SUBLANES, d), lambda i, j: (jnp.minimum((i + 1) * hb, last), 0)),
                  pl.BlockSpec((1, d), lambda i, j: (0, 0)),
                  mod_spec, mod_spec,
                  pl.BlockSpec((d, tn), lambda i, j: (0, j)),
                  pl.BlockSpec((d, ndt), lambda i, j: (0, n // ndt)),
                  pl.BlockSpec((3, tn), lambda i, j: (0, jnp.maximum(j - n_plain, 0))),
                  pl.BlockSpec((1, tn), lambda i, j: (0, jnp.maximum(j - n_plain, 0)))],
        out_specs=[pl.BlockSpec((tm, tn), lambda i, j: (i, j)),
                   pl.BlockSpec((tm, ndt), lambda i, j: (i, 0))],
        out_shape=[jax.ShapeDtypeStruct((t, n), BF16), jax.ShapeDtypeStruct((t, ndt), F32)],
        scratch_shapes=[pltpu.VMEM((tm + 2 * SUBLANES, d), BF16)],
        compiler_params=_params("arbitrary", "arbitrary"),
        name="m2_in_proj",
    )(x2, x2, x2, g, scale, shift, w_in, w_in, conv_w, conv_b)


N_PIECES = 3
GROUP_LANES = 128


def _piece_base(heads_per_group):
    w = N_PIECES * heads_per_group
    return {"cum_f": 0, "cum_b": w, "ecum_f": 2 * w, "ecum_b": 3 * w, "wend_f": 4 * w}


def _perm_matrices(n_heads, heads_per_group):
    rr = heads_per_group
    base = _piece_base(rr)
    nat = 2 * n_heads
    perm_a = np.zeros((3 * N_PIECES * nat, N_GROUPS * GROUP_LANES), np.float32)
    perm_w = np.zeros((N_PIECES * nat, N_GROUPS * GROUP_LANES), np.float32)
    for piece in range(N_PIECES):
        for d in range(2):
            for h in range(n_heads):
                g, r = divmod(h, rr)
                lane = piece * nat + d * n_heads + h
                col = g * GROUP_LANES + piece * rr + r
                perm_a[0 * N_PIECES * nat + lane, col + (base["cum_f"] if d == 0 else base["cum_b"])] = 1
                perm_a[1 * N_PIECES * nat + lane, col + (base["ecum_f"] if d == 0 else base["ecum_b"])] = 1
                if d == 0:
                    perm_a[2 * N_PIECES * nat + lane, col + base["wend_f"]] = 1
                else:
                    perm_w[lane, col] = 1
    return jnp.asarray(perm_a, BF16), jnp.asarray(perm_w, BF16)


def _select_matrices(heads_per_group, head_dim, q):
    rr = heads_per_group
    base = _piece_base(rr)
    gw = rr * head_dim
    sel_c = np.zeros((GROUP_LANES, 2 * rr * q), np.float32)
    sel_e = np.zeros((GROUP_LANES, 3 * gw), np.float32)
    sel_w = np.zeros((GROUP_LANES, gw), np.float32)
    for piece in range(N_PIECES):
        for r in range(rr):
            k = piece * rr + r
            sel_c[base["cum_f"] + k, r * q:(r + 1) * q] = 1
            sel_c[base["cum_b"] + k, (rr + r) * q:(rr + r + 1) * q] = 1
            sel_e[base["ecum_f"] + k, r * head_dim:(r + 1) * head_dim] = 1
            sel_e[base["ecum_b"] + k, gw + r * head_dim:gw + (r + 1) * head_dim] = 1
            sel_e[base["wend_f"] + k, 2 * gw + r * head_dim:2 * gw + (r + 1) * head_dim] = 1
            sel_w[k, r * head_dim:(r + 1) * head_dim] = 1
    return jnp.asarray(sel_c, BF16), jnp.asarray(sel_e, BF16), jnp.asarray(sel_w, BF16)


def _bf16_pieces(v):
    hi = v.astype(BF16)
    rem = v - hi.astype(F32)
    mid = rem.astype(BF16)
    lo = (rem - mid.astype(F32)).astype(BF16)
    return [hi, mid, lo]


def _dt_kernel(raw_ref, bias_ref, alog_ref, perm_a_ref, perm_w_ref,
               a_ref, w_ref, rowt_ref, diagt_ref, elast_ref, *, n_heads):
    q = CHUNK
    w = raw_ref.shape[1]
    r = lax.broadcasted_iota(jnp.int32, (q, q), 0)
    c = lax.broadcasted_iota(jnp.int32, (q, q), 1)
    lower = jnp.where(c <= r, 1.0, 0.0).astype(BF16)
    upper = jnp.where(c >= r, 1.0, 0.0).astype(BF16)
    fwd = lax.broadcasted_iota(jnp.int32, (q, w), 1) < n_heads
    neg_a2 = jnp.exp(alog_ref[...]) * LOG2_E
    lhs_a, lhs_w = [], []
    for k in range(raw_ref.shape[0] // q):
        rows = slice(k * q, (k + 1) * q)
        v = raw_ref[rows, :] + bias_ref[...]
        dt = jnp.maximum(v, 0.0) + jnp.log1p(jnp.exp(-jnp.abs(v)))
        a2 = -(dt * neg_a2)
        pieces = jnp.concatenate(_bf16_pieces(a2), axis=1)
        pre = _dot(lower, pieces)
        suf = _dot(upper, pieces)
        cum = jnp.where(fwd, pre[:, :w] + pre[:, w:2 * w] + pre[:, 2 * w:],
                        suf[:, :w] + suf[:, w:2 * w] + suf[:, 2 * w:])
        last = jnp.where(fwd[0:1, :], cum[q - 1:q, :], cum[0:1, :])
        wend = _bf16_pieces(jnp.exp2(last - cum) * dt)
        lhs_a.append(jnp.concatenate(_bf16_pieces(cum) + _bf16_pieces(jnp.exp2(cum)) + wend, axis=1))
        lhs_w.append(jnp.concatenate(wend, axis=1))
        rowt_ref[0, :, rows] = (cum - jnp.log2(dt)).T
        diagt_ref[0, :, rows] = jnp.log2(dt + pltpu.roll(dt, n_heads, 1)).T
        elast_ref[k] = jnp.exp2(last)
    a_ref[...] = _dot(jnp.concatenate(lhs_a, axis=0), perm_a_ref[...]).astype(BF16)
    w_ref[...] = _dot(jnp.concatenate(lhs_w, axis=0), perm_w_ref[...]).astype(BF16)


def _dt_prep(dt_raw, dt_bias, a_log, bsz, seq, heads_per_group):
    t, w = dt_raw.shape
    q = CHUNK
    nc = seq // q
    perm_a, perm_w = _perm_matrices(w // 2, heads_per_group)
    gl = N_GROUPS * GROUP_LANES
    per_step = DT_CHUNKS_PER_STEP
    assert nc % per_step == 0
    rows = per_step * q
    nblk = nc // per_step
    col_spec = pl.BlockSpec((rows, w), lambda i: (i, 0))
    vec_spec = pl.BlockSpec((1, w), lambda i: (0, 0))
    grp_spec = pl.BlockSpec((rows, gl), lambda i: (i, 0))
    return pl.pallas_call(
        functools.partial(_dt_kernel, n_heads=w // 2),
        grid=(t // rows,),
        in_specs=[col_spec, vec_spec, vec_spec,
                  pl.BlockSpec(perm_a.shape, lambda i: (0, 0)),
                  pl.BlockSpec(perm_w.shape, lambda i: (0, 0))],
        out_specs=[grp_spec, grp_spec,
                   pl.BlockSpec((1, w, rows), lambda i: (i // nblk, 0, i % nblk)),
                   pl.BlockSpec((1, w, rows), lambda i: (i // nblk, 0, i % nblk)),
                   pl.BlockSpec((per_step, 1, w), lambda i: (i, 0, 0))],
        out_shape=[jax.ShapeDtypeStruct((t, gl), BF16), jax.ShapeDtypeStruct((t, gl), BF16),
                   jax.ShapeDtypeStruct((bsz, w, seq), F32), jax.ShapeDtypeStruct((bsz, w, seq), F32),
                   jax.ShapeDtypeStruct((t // q, 1, w), F32)],
        compiler_params=_params("arbitrary"),
        name="m2_dt_prep",
    )(dt_raw, dt_bias, a_log, perm_a, perm_w)


def _ssd_backward_block(blk, x_ref, b_ref, w_ref, selw_ref, el_ref, stb, sb_scr, *, q, per_step):
    ks = list(reversed(range(per_step)))
    rows = [slice(k * q, (k + 1) * q) for k in range(per_step)]
    wexp = {k: _dot(w_ref[rows[k], :], selw_ref[...]) for k in ks}
    upd = {}
    for k in ks:
        xw = (x_ref[rows[k], :].astype(F32) * wexp[k]).astype(BF16)
        bt = b_ref[rows[k], :].astype(F32).T.astype(BF16)
        upd[k] = _dot(bt, xw)
    st = stb[...]
    for k in ks:
        sb_scr[blk * per_step + k] = st.astype(BF16)
        st = st * el_ref[0, k, 1:2, :] + upd[k]
    stb[...] = st


def _ssd_forward_block(blk, x_ref, b_ref, c_ref, z_ref, a_ref, selc_ref, sele_ref, rtf_ref, rtb_ref, dg_ref,
                       el_ref, dsk_ref, ng_ref, u_ref, stf, sb_scr, *, q, per_step, heads_per_group, head_dim):
    rr = heads_per_group
    gw = rr * head_dim
    ks = list(range(per_step))
    rows = [slice(k * q, (k + 1) * q) for k in ks]
    t_idx = lax.broadcasted_iota(jnp.int32, (q, q), 0)
    s_idx = lax.broadcasted_iota(jnp.int32, (q, q), 1)
    below = t_idx > s_idx
    above = t_idx < s_idx
    lane_head = lax.broadcasted_iota(jnp.int32, (q, 256), 1) // head_dim
    n_pairs = rr // 2
    heads_per_tile = 256 // head_dim

    scores, col_f, col_b, exp_f, exp_b, exp_w = {}, {}, {}, {}, {}, {}
    for k in ks:
        a = a_ref[rows[k], :]
        scores[k] = lax.dot_general(c_ref[rows[k], :], b_ref[rows[k], :], (((1,), (1,)), ((), ())),
                                    preferred_element_type=F32)
        for pr in range(n_pairs):
            col_f[k, pr] = _dot(a, selc_ref[:, 2 * pr * q:(2 * pr + 2) * q])
            col_b[k, pr] = _dot(a, selc_ref[:, (rr + 2 * pr) * q:(rr + 2 * pr + 2) * q])
        exp_f[k] = _dot(a, sele_ref[:, 0:gw])
        exp_b[k] = _dot(a, sele_ref[:, gw:2 * gw])
        exp_w[k] = _dot(a, sele_ref[:, 2 * gw:3 * gw])

    y, upd = {}, {}
    for k in ks:
        x = x_ref[rows[k], :]
        rtf, rtb, dg = rtf_ref[0, :, rows[k]], rtb_ref[0, :, rows[k]], dg_ref[0, :, rows[k]]
        y_parts = []
        for tile in range(gw // 256):
            xt = x[:, tile * 256:(tile + 1) * 256]
            acc = None
            for pair in range(heads_per_tile // 2):
                pr = tile * (heads_per_tile // 2) + pair
                ms, xs = [], []
                for j in range(2):
                    h = 2 * pr + j
                    cols = slice(j * q, (j + 1) * q)
                    seg = jnp.where(below, col_f[k, pr][:, cols] - rtf[h:h + 1, :],
                                    jnp.where(above, col_b[k, pr][:, cols] - rtb[h:h + 1, :], dg[h:h + 1, :]))
                    ms.append((scores[k] * jnp.exp2(seg)).astype(BF16))
                    xs.append(jnp.where(lane_head == 2 * pair + j, xt, jnp.zeros_like(xt)))
                part = _dot(jnp.concatenate(ms, axis=1), jnp.concatenate(xs, axis=0))
                acc = part if acc is None else acc + part
            y_parts.append(acc)
        xf32 = x.astype(F32)
        y[k] = (jnp.concatenate(y_parts, axis=1) + dsk_ref[...] * xf32
                + exp_b[k] * _dot(c_ref[rows[k], :], sb_scr[blk * per_step + k]))
        xw = (xf32 * exp_w[k]).astype(BF16)
        bt = b_ref[rows[k], :].astype(F32).T.astype(BF16)
        upd[k] = _dot(bt, xw)

    st = stf[...]
    sts = {}
    for k in ks:
        sts[k] = st.astype(BF16)
        st = st * el_ref[0, k, 0:1, :] + upd[k]
    stf[...] = st
    for k in ks:
        yk = y[k] + exp_f[k] * _dot(c_ref[rows[k], :], sts[k])
        u = yk * _silu(z_ref[rows[k], :].astype(F32))
        u_ref[rows[k], :] = _rms(u, ng_ref[...]).astype(BF16)


def _ssd_kernel(x_ref, b_ref, c_ref, z_ref, a_ref, w_ref, selc_ref, sele_ref, selw_ref,
                rtf_ref, rtb_ref, dg_ref, el_ref, dsk_ref, ng_ref,
                u_ref, stf, stb, sb_scr, *, heads_per_group, head_dim, per_step):
    q = x_ref.shape[0] // per_step
    phase = pl.program_id(2)
    step = pl.program_id(3)
    nblk = pl.num_programs(3)

    @pl.when(phase == 0)
    def _():
        @pl.when(step == 0)
        def _():
            stb[...] = jnp.zeros_like(stb)

        _ssd_backward_block(nblk - 1 - step, x_ref, b_ref, w_ref, selw_ref, el_ref, stb, sb_scr,
                            q=q, per_step=per_step)

    @pl.when(phase == 1)
    def _():
        @pl.when(step == 0)
        def _():
            stf[...] = jnp.zeros_like(stf)

        _ssd_forward_block(step, x_ref, b_ref, c_ref, z_ref, a_ref, selc_ref, sele_ref, rtf_ref, rtb_ref, dg_ref,
                           el_ref, dsk_ref, ng_ref, u_ref, stf, sb_scr, q=q, per_step=per_step,
                           heads_per_group=heads_per_group, head_dim=head_dim)


def _ssd(zxbc, a_cols, w_cols, row_t, diag_t, elast, dskip, norm_g, bsz, seq, d_inner, d_state, n_heads,
         to_cast):
    t = zxbc.shape[0]
    q = CHUNK
    nc = seq // q
    per_step = SSD_CHUNKS_PER_STEP
    assert nc % per_step == 0
    nblk = nc // per_step
    rows = per_step * q
    g = N_GROUPS
    rr = n_heads // g
    p = d_inner // n_heads
    gw = rr * p
    assert p == 64 and d_state == 128 and q == 128 and gw % 256 == 0 and rr % 2 == 0
    assert 5 * N_PIECES * rr <= GROUP_LANES and d_inner // g == gw
    xoff = d_inner // gw
    boff = 2 * d_inner // d_state
    sel_c, sel_e, sel_w = _select_matrices(rr, p, q)
    elast = elast.reshape(bsz * nblk, per_step, 2, d_inner)

    def both(b, gi, ph, s):
        return b * nblk + ph * s + (1 - ph) * (nblk - 1 - s)

    def fwd(b, gi, ph, s):
        return b * nblk + ph * s

    def bwd(b, gi, ph, s):
        return b * nblk + (1 - ph) * (nblk - 1 - s)

    const = lambda arr: pl.BlockSpec(arr.shape, lambda b, gi, ph, s: (0, 0))
    grid = (bsz, g, 2, nblk)
    cast_in, cast_out, cast_shape = _cast_args(to_cast, bsz * g * 2 * nblk,
                                               lambda b, gi, ph, s: ((b * g + gi) * 2 + ph) * nblk + s)
    outs = pl.pallas_call(
        _with_casts(functools.partial(_ssd_kernel, heads_per_group=rr, head_dim=p, per_step=per_step),
                    15, 1, len(to_cast)),
        grid=grid,
        in_specs=[pl.BlockSpec((rows, gw), lambda b, gi, ph, s: (both(b, gi, ph, s), xoff + gi)),
                  pl.BlockSpec((rows, d_state), lambda b, gi, ph, s: (both(b, gi, ph, s), boff + gi)),
                  pl.BlockSpec((rows, d_state), lambda b, gi, ph, s: (fwd(b, gi, ph, s), boff + g + gi)),
                  pl.BlockSpec((rows, gw), lambda b, gi, ph, s: (fwd(b, gi, ph, s), gi)),
                  pl.BlockSpec((rows, GROUP_LANES), lambda b, gi, ph, s: (fwd(b, gi, ph, s), gi)),
                  pl.BlockSpec((rows, GROUP_LANES), lambda b, gi, ph, s: (bwd(b, gi, ph, s), gi)),
                  const(sel_c), const(sel_e), const(sel_w),
                  pl.BlockSpec((1, rr, rows), lambda b, gi, ph, s: (b, gi, ph * s)),
                  pl.BlockSpec((1, rr, rows), lambda b, gi, ph, s: (b, g + gi, ph * s)),
                  pl.BlockSpec((1, rr, rows), lambda b, gi, ph, s: (b, gi, ph * s)),
                  pl.BlockSpec((1, per_step, 2, gw), lambda b, gi, ph, s: (both(b, gi, ph, s), 0, 0, gi)),
                  pl.BlockSpec((1, gw), lambda b, gi, ph, s: (0, gi)),
                  pl.BlockSpec((1, gw), lambda b, gi, ph, s: (0, gi))] + cast_in,
        out_specs=[pl.BlockSpec((rows, gw), lambda b, gi, ph, s: (fwd(b, gi, ph, s), gi))] + cast_out,
        out_shape=[jax.ShapeDtypeStruct((t, d_inner), BF16)] + cast_shape,
        scratch_shapes=[pltpu.VMEM((d_state, gw), F32), pltpu.VMEM((d_state, gw), F32),
                        pltpu.VMEM((nc, d_state, gw), BF16)],
        compiler_params=_params("arbitrary", "arbitrary", "arbitrary", "arbitrary"),
        name="m2_ssd",
    )(zxbc, zxbc, zxbc, zxbc, a_cols, w_cols, sel_c, sel_e, sel_w, row_t, row_t, diag_t, elast, dskip, norm_g,
      *[spec[0] for spec in to_cast])
    return outs[0], outs[1:]


def _m2_out_kernel(u_ref, w_ref, x_ref, gate_ref, o_ref):
    o_ref[...] = x_ref[...] + gate_ref[0] * _dot(u_ref[...], w_ref[...])


def _m2_out(u, w_out, x2, gate, seq, tm, tn):
    t, d = x2.shape
    di = u.shape[1]
    tps = seq // tm
    return pl.pallas_call(
        _m2_out_kernel,
        grid=(t // tm, d // tn),
        in_specs=[pl.BlockSpec((tm, di), lambda i, j: (i, 0)),
                  pl.BlockSpec((di, tn), lambda i, j: (0, j)),
                  pl.BlockSpec((tm, tn), lambda i, j: (i, j)),
                  pl.BlockSpec((1, 1, tn), lambda i, j: (i // tps, 0, j))],
        out_specs=pl.BlockSpec((tm, tn), lambda i, j: (i, j)),
        out_shape=jax.ShapeDtypeStruct((t, d), F32),
        compiler_params=_params("arbitrary", "arbitrary"),
        name="m2_out_proj",
    )(u, w_out, x2, gate)


def _row_tile(seq, want):
    tm = min(want, seq)
    assert seq % tm == 0 and tm % HALO_ROWS == 0
    return tm


def _tiles(seq):
    return dict(tm=_row_tile(seq, 1024), tm_conv=_row_tile(seq, 512), tn=512, tn_wide=1024, tf=1024)


def kernel(x, c, ada_w, ada_b, norm_g, final_g, sc_in_w, sc_conv_w, sc_out_w, m2_in_w, m2_conv_w, m2_conv_b,
           m2_dt_bias, m2_a_log, m2_d, m2_norm_g, m2_out_w, mlp_w1, mlp_w2):
    bsz, seq, d = x.shape
    t = bsz * seq
    d_inner = m2_norm_g.shape[1]
    n_heads = m2_d.shape[1]
    conv_dim = m2_conv_w.shape[1]
    d_state = (conv_dim - d_inner) // (2 * N_GROUPS)
    head_dim = d_inner // n_heads
    assert seq % CHUNK == 0
    ts = _tiles(seq)
    tm, tn, tf = ts["tm"], ts["tn"], ts["tf"]

    mods = _modulation(c, ada_w, ada_b)
    x2 = x.reshape(t, d)
    vec = lambda v: v.reshape(1, -1)

    shift, scale, gate = mods[0]
    bg, cv, (sc_out_bf, w1_bf, w2_bf) = _sc_in(x2, vec(norm_g[0, 0]), scale, shift, sc_in_w[0].astype(BF16),
                                               seq, tm, tn, [(sc_out_w, 0, None), (mlp_w1, 0, tf), (mlp_w2, 0, None)])
    x2 = _sc_out(cv, bg, sc_conv_w[0].T, sc_out_bf, x2, gate, seq, ts["tm_conv"])
    shift, scale, gate = mods[1]
    x2, (m2_in_bf,) = _mlp(x2, vec(norm_g[0, 1]), scale, shift, gate, w1_bf, w2_bf,
                           vec(final_g), seq, tm, final_norm=False, to_cast=[(m2_in_w, 0, None)])

    shift, scale, gate = mods[2]
    zxbc, dt_raw = _m2_in(x2, vec(norm_g[1, 0]), scale, shift, m2_in_bf, m2_conv_w[0].T, vec(m2_conv_b[0]),
                          d_inner, seq, tm, ts["tn_wide"])
    rr = n_heads // N_GROUPS
    a_cols, w_cols, row_t, diag_t, elast = _dt_prep(dt_raw, vec(m2_dt_bias[0]), vec(m2_a_log[0]), bsz, seq, rr)
    elast = jnp.repeat(elast.reshape(-1, 2, n_heads), head_dim, axis=2)
    dskip = jnp.repeat(m2_d[0], head_dim).reshape(1, d_inner)
    u, (m2_out_bf, w1_bf, w2_bf) = _ssd(zxbc, a_cols, w_cols, row_t, diag_t, elast, dskip, vec(m2_norm_g[0]),
                                        bsz, seq, d_inner, d_state, n_heads,
                                        [(m2_out_w, 0, None), (mlp_w1, 1, tf), (mlp_w2, 1, None)])
    x2 = _m2_out(u, m2_out_bf, x2, gate, seq, tm, ts["tn_wide"])
    shift, scale, gate = mods[3]
    x2, _ = _mlp(x2, vec(norm_g[1, 1]), scale, shift, gate, w1_bf, w2_bf,
                 vec(final_g), seq, tm, final_norm=True)
    return x2.reshape(bsz, seq, d)
```

```python
import functools

import numpy as np
import jax
import jax.numpy as jnp
from jax import lax
from jax.experimental import pallas as pl
from jax.experimental.pallas import tpu as pltpu

N_GROUPS = 8
CHUNK = 128
MLP_EPILOGUE_SLABS = 4
CONV_COLUMN_PARTS = 2
DT_CHUNKS_PER_STEP = 8
SSD_CHUNKS_PER_STEP = 16
EPS = 1e-6
LOG2_E = 1.4426950408889634
SUBLANES = 8
LANES = 128
HALO_ROWS = 16
VMEM_LIMIT_BYTES = 62 * 1024 * 1024

F32 = jnp.float32
BF16 = jnp.bfloat16


def _params(*sem):
    return pltpu.CompilerParams(dimension_semantics=sem, vmem_limit_bytes=VMEM_LIMIT_BYTES)


def _dot(a, b):
    return jnp.dot(a, b, preferred_element_type=F32)


def _silu_of_half(half):
    return half * jnp.tanh(half) + half


def _silu(x):
    return _silu_of_half(0.5 * x)


def _rms(x, g):
    ms = jnp.mean(x * x, axis=-1, keepdims=True)
    return (x * lax.rsqrt(ms + EPS)) * g


def _modnorm(x, g, scale, shift):
    return _rms(x, g) * (1.0 + scale) + shift


def _conv3(main, prev_row, next_row, w):
    tm, width = main.shape
    core = w[0:1, :] * pltpu.roll(main, 1, 0) + w[1:2, :] * main + w[2:3, :] * pltpu.roll(main, tm - 1, 0)
    row = lax.broadcasted_iota(jnp.int32, (SUBLANES, width), 0)
    first = jnp.where(row == 0, w[0:1, :] * (prev_row - main[tm - 1:tm, :]), 0.0)
    last = jnp.where(row == SUBLANES - 1, w[2:3, :] * (next_row - main[0:1, :]), 0.0)
    return jnp.concatenate([core[:SUBLANES] + first, core[SUBLANES:tm - SUBLANES], core[tm - SUBLANES:] + last],
                           axis=0)


def _halo_rows(prev_ref, next_ref, tiles_per_seq):
    i = pl.program_id(0) % tiles_per_seq
    prev = prev_ref[...].astype(F32)[HALO_ROWS - 1:HALO_ROWS, :]
    nxt = next_ref[...].astype(F32)[0:1, :]
    prev = jnp.where(i == 0, 0.0, prev)
    nxt = jnp.where(i == tiles_per_seq - 1, 0.0, nxt)
    return prev, nxt


def _with_casts(body, n_in, n_out, n_cast):
    def kern(*refs):
        ins, rest = refs[:n_in], refs[n_in:]
        cast_in, rest = rest[:n_cast], rest[n_cast:]
        outs, rest = rest[:n_out], rest[n_out:]
        cast_out, scratch = rest[:n_cast], rest[n_cast:]
        for src, dst in zip(cast_in, cast_out):
            if len(dst.shape) == 3:
                width = dst.shape[2]
                for jb in range(dst.shape[0]):
                    dst[jb] = src[:, jb * width:(jb + 1) * width].astype(BF16)
            else:
                dst[...] = src[...].astype(BF16)
        body(*ins, *outs, *scratch)
    return kern


def _cast_specs(stacked, layer, col_block, n_steps, lin):
    _, r, c = stacked.shape
    nb = n_steps
    while r % nb or (r // nb) % HALO_ROWS:
        nb //= 2
    blk = lambda *idx: lin(*idx) * nb // n_steps
    src = pl.BlockSpec((None, r // nb, c), lambda *idx: (layer, blk(*idx), 0))
    if col_block is None:
        return src, pl.BlockSpec((r // nb, c), lambda *idx: (blk(*idx), 0)), jax.ShapeDtypeStruct((r, c), BF16)
    ncb = c // col_block
    dst = pl.BlockSpec((ncb, r // nb, col_block), lambda *idx: (0, blk(*idx), 0))
    return src, dst, jax.ShapeDtypeStruct((ncb, r, col_block), BF16)


def _cast_args(to_cast, n_steps, lin):
    specs = [_cast_specs(w, layer, col_block, n_steps, lin) for w, layer, col_block in to_cast]
    return [s[0] for s in specs], [s[1] for s in specs], [s[2] for s in specs]


def _mod_kernel(c_ref, w_ref, b_ref, o_ref, cond_scr):
    @pl.when((pl.program_id(0) == 0) & (pl.program_id(1) == 0))
    def _():
        cond_scr[...] = _silu(c_ref[...])

    for slab in range(w_ref.shape[2] // LANES):
        cols = slice(slab * LANES, (slab + 1) * LANES)
        w = w_ref[0, :, cols]
        for b in range(c_ref.shape[0]):
            o_ref[0, b:b + 1, cols] = jnp.sum(w * cond_scr[b], axis=0, keepdims=True) + b_ref[0, :, cols]


def _modulation(c, ada_w, ada_b):
    bsz, d = c.shape
    n_mod = ada_w.shape[0] * ada_w.shape[1]
    w = ada_w.reshape(n_mod, d, 3 * d)
    b = ada_b.reshape(n_mod, 1, 3 * d)
    c_rep = jnp.broadcast_to(c[:, :, None], (bsz, d, LANES))
    tn = 1536
    out = pl.pallas_call(
        _mod_kernel,
        grid=(n_mod, 3 * d // tn),
        in_specs=[pl.BlockSpec((bsz, d, LANES), lambda m, j: (0, 0, 0)),
                  pl.BlockSpec((1, d, tn), lambda m, j: (m, 0, j)),
                  pl.BlockSpec((1, 1, tn), lambda m, j: (m, 0, j))],
        out_specs=pl.BlockSpec((1, bsz, tn), lambda m, j: (m, 0, j)),
        out_shape=jax.ShapeDtypeStruct((n_mod, bsz, 3 * d), F32),
        scratch_shapes=[pltpu.VMEM((bsz, d, LANES), F32)],
        compiler_params=_params("arbitrary", "arbitrary"),
        name="modulation",
    )(c_rep, w, b)
    out = out.reshape(n_mod, bsz, 3, 1, d)
    return [(out[m, :, 0], out[m, :, 1], out[m, :, 2]) for m in range(n_mod)]


def _sc_in_kernel(x_ref, g_ref, sc_ref, sh_ref, wb_ref, wc_ref, wv_ref, b_out, cv_out, h_scr):
    def project(h):
        b_out[...] = _dot(h, wb_ref[...]).astype(BF16)
        cv_out[...] = (_dot(h, wc_ref[...]) * _dot(h, wv_ref[...])).astype(BF16)

    @pl.when(pl.program_id(1) == 0)
    def _():
        h = _modnorm(x_ref[...], g_ref[...], sc_ref[0], sh_ref[0]).astype(BF16)
        h_scr[...] = h
        project(h)

    @pl.when(pl.program_id(1) != 0)
    def _():
        project(h_scr[...])


def _sc_in(x2, g, scale, shift, w_in, seq, tm, tn, to_cast):
    t, d = x2.shape
    tps = seq // tm
    nb = d // tn
    grid = (t // tm, nb)
    cast_in, cast_out, cast_shape = _cast_args(to_cast, grid[0] * grid[1], lambda i, j: i * nb + j)
    row_spec = pl.BlockSpec((tm, d), lambda i, j: (i, 0))
    vec_spec = pl.BlockSpec((1, d), lambda i, j: (0, 0))
    mod_spec = pl.BlockSpec((1, 1, d), lambda i, j: (i // tps, 0, 0))
    out_spec = pl.BlockSpec((tm, tn), lambda i, j: (i, j))
    outs = pl.pallas_call(
        _with_casts(_sc_in_kernel, 7, 2, len(to_cast)),
        grid=grid,
        in_specs=[row_spec, vec_spec, mod_spec, mod_spec,
                  pl.BlockSpec((d, tn), lambda i, j: (0, j)),
                  pl.BlockSpec((d, tn), lambda i, j: (0, j + nb)),
                  pl.BlockSpec((d, tn), lambda i, j: (0, j + 2 * nb))] + cast_in,
        out_specs=[out_spec, out_spec] + cast_out,
        out_shape=[jax.ShapeDtypeStruct((t, d), BF16), jax.ShapeDtypeStruct((t, d), BF16)] + cast_shape,
        scratch_shapes=[pltpu.VMEM((tm, d), BF16)],
        compiler_params=_params("arbitrary", "arbitrary"),
        name="sc_in_proj",
    )(x2, g, scale, shift, w_in, w_in, w_in, *[spec[0] for spec in to_cast])
    return outs[0], outs[1], outs[2:]


def _sc_out_kernel(cv_ref, cvp_ref, cvn_ref, bg_ref, cw_ref, w_ref, x_ref, gate_ref, o_ref, *, tiles_per_seq):
    prev, nxt = _halo_rows(cvp_ref, cvn_ref, tiles_per_seq)
    u = _conv3(cv_ref[...].astype(F32), prev, nxt, cw_ref[...])
    u = (u * bg_ref[...].astype(F32)).astype(BF16)
    o_ref[...] = x_ref[...] + gate_ref[0] * _dot(u, w_ref[...])


def _sc_out(cv, bg, conv_w, w_out, x2, gate, seq, tm):
    t, d = x2.shape
    tps = seq // tm
    hb = tm // HALO_ROWS
    last = t // HALO_ROWS - 1
    row_spec = pl.BlockSpec((tm, d), lambda i: (i, 0))
    return pl.pallas_call(
        functools.partial(_sc_out_kernel, tiles_per_seq=tps),
        grid=(t // tm,),
        in_specs=[row_spec,
                  pl.BlockSpec((HALO_ROWS, d), lambda i: (jnp.maximum(i * hb - 1, 0), 0)),
                  pl.BlockSpec((HALO_ROWS, d), lambda i: (jnp.minimum((i + 1) * hb, last), 0)),
                  row_spec,
                  pl.BlockSpec((3, d), lambda i: (0, 0)),
                  pl.BlockSpec((d, d), lambda i: (0, 0)),
                  row_spec,
                  pl.BlockSpec((1, 1, d), lambda i: (i // tps, 0, 0))],
        out_specs=row_spec,
        out_shape=jax.ShapeDtypeStruct((t, d), F32),
        compiler_params=_params("arbitrary"),
        name="sc_conv_out_proj",
    )(cv, cv, cv, bg, conv_w, w_out, x2, gate)


def _mlp_kernel(x_ref, g_ref, sc_ref, sh_ref, gate_ref, w1_ref, w2_ref, fg_ref, o_ref, h_scr, *, final_norm):
    k = pl.program_id(1)

    def hidden_chunk(h):
        a = jnp.maximum(_dot(h, w1_ref[...]), 0.0)
        return _dot((a * a).astype(BF16), w2_ref[...])

    @pl.when(k == 0)
    def _():
        h = _modnorm(x_ref[...], g_ref[...], sc_ref[0], sh_ref[0]).astype(BF16)
        h_scr[...] = h
        o_ref[...] = hidden_chunk(h)

    last = pl.num_programs(1) - 1

    @pl.when((k != 0) & (k != last))
    def _():
        o_ref[...] += hidden_chunk(h_scr[...])

    @pl.when(k == last)
    def _():
        o_ref[...] += hidden_chunk(h_scr[...])
        slab = x_ref.shape[0] // MLP_EPILOGUE_SLABS
        for r in range(MLP_EPILOGUE_SLABS):
            rows = slice(r * slab, (r + 1) * slab)
            y = x_ref[rows, :] + gate_ref[0] * o_ref[rows, :]
            if final_norm:
                y = _rms(y, fg_ref[...])
            o_ref[rows, :] = y


def _mlp(x2, g, scale, shift, gate, w1, w2, final_g, seq, tm, final_norm, to_cast=()):
    t, d = x2.shape
    nk, _, tf = w1.shape
    tps = seq // tm
    assert nk >= 2
    grid = (t // tm, nk)
    cast_in, cast_out, cast_shape = _cast_args(to_cast, grid[0] * grid[1], lambda i, k: i * nk + k)
    row_spec = pl.BlockSpec((tm, d), lambda i, k: (i, 0))
    vec_spec = pl.BlockSpec((1, d), lambda i, k: (0, 0))
    mod_spec = pl.BlockSpec((1, 1, d), lambda i, k: (i // tps, 0, 0))
    outs = pl.pallas_call(
        _with_casts(functools.partial(_mlp_kernel, final_norm=final_norm), 8, 1, len(to_cast)),
        grid=grid,
        in_specs=[row_spec, vec_spec, mod_spec, mod_spec, mod_spec,
                  pl.BlockSpec((None, d, tf), lambda i, k: (k, 0, 0)),
                  pl.BlockSpec((tf, d), lambda i, k: (k, 0)),
                  vec_spec] + cast_in,
        out_specs=[row_spec] + cast_out,
        out_shape=[jax.ShapeDtypeStruct((t, d), F32)] + cast_shape,
        scratch_shapes=[pltpu.VMEM((tm, d), BF16)],
        compiler_params=_params("arbitrary", "arbitrary"),
        name="mlp_final" if final_norm else "mlp",
    )(x2, g, scale, shift, gate, w1, w2, final_g, *[spec[0] for spec in to_cast])
    return outs[0], outs[1:]


def _m2_in_kernel(x_ref, xp_ref, xn_ref, g_ref, sc_ref, sh_ref, w_ref, wdt_ref, cw_ref, cb_ref,
                  o_ref, dt_ref, h_scr, shift_scr, *, tiles_per_seq, n_plain):
    j = pl.program_id(1)
    tm = x_ref.shape[0]

    @pl.when(j == 0)
    def _():
        h = _modnorm(x_ref[...], g_ref[...], sc_ref[0], sh_ref[0]).astype(BF16)
        h_scr[:tm, :] = h
        halo = jnp.concatenate([xp_ref[...], xn_ref[...]], axis=0)
        h_scr[tm:, :] = _modnorm(halo, g_ref[...], sc_ref[0], sh_ref[0]).astype(BF16)
        dt_ref[...] = _dot(h, wdt_ref[...])
        o_ref[...] = _dot(h, w_ref[...]).astype(BF16)

    @pl.when((j != 0) & (j < n_plain))
    def _():
        o_ref[...] = _dot(h_scr[:tm, :], w_ref[...]).astype(BF16)

    @pl.when(j >= n_plain)
    def _():
        i = pl.program_id(0) % tiles_per_seq
        width = w_ref.shape[1] // CONV_COLUMN_PARTS
        for part in range(CONV_COLUMN_PARTS):
            cols = slice(part * width, (part + 1) * width)
            proj = _dot(h_scr[...], w_ref[:, cols])
            prev = jnp.where(i == 0, 0.0, proj[tm + SUBLANES - 1:tm + SUBLANES, :])
            nxt = jnp.where(i == tiles_per_seq - 1, 0.0, proj[tm + SUBLANES:tm + SUBLANES + 1, :])
            shift_scr[SUBLANES:SUBLANES + tm, :] = proj[:tm, :]
            shift_scr[SUBLANES - 1:SUBLANES, :] = prev
            shift_scr[SUBLANES + tm:SUBLANES + tm + 1, :] = nxt
            w = 0.5 * cw_ref[:, cols]
            half = (w[0:1, :] * shift_scr[SUBLANES - 1:SUBLANES - 1 + tm, :] + w[1:2, :] * proj[:tm, :]
                    + w[2:3, :] * shift_scr[SUBLANES + 1:SUBLANES + 1 + tm, :] + 0.5 * cb_ref[:, cols])
            o_ref[:, cols] = _silu_of_half(half).astype(BF16)


def _m2_in(x2, g, scale, shift, w_in, conv_w, conv_b, n_plain_cols, seq, tm, tn):
    t, d = x2.shape
    n = n_plain_cols + conv_w.shape[1]
    ndt = w_in.shape[1] - n
    assert n_plain_cols % tn == 0 and n % tn == 0 and n % ndt == 0
    n_plain = n_plain_cols // tn
    tps = seq // tm
    hb = tm // SUBLANES
    last = t // SUBLANES - 1
    mod_spec = pl.BlockSpec((1, 1, d), lambda i, j: (i // tps, 0, 0))
    return pl.pallas_call(
        functools.partial(_m2_in_kernel, tiles_per_seq=tps, n_plain=n_plain),
        grid=(t // tm, n // tn),
        in_specs=[pl.BlockSpec((tm, d), lambda i, j: (i, 0)),
                  pl.BlockSpec((SUBLANES, d), lambda i, j: (jnp.maximum(i * hb - 1, 0), 0)),
                  pl.BlockSpec((SUBLANES, d), lambda i, j: (jnp.minimum((i + 1) * hb, last), 0)),
                  pl.BlockSpec((1, d), lambda i, j: (0, 0)),
                  mod_spec, mod_spec,
                  pl.BlockSpec((d, tn), lambda i, j: (0, j)),
                  pl.BlockSpec((d, ndt), lambda i, j: (0, n // ndt)),
                  pl.BlockSpec((3, tn), lambda i, j: (0, jnp.maximum(j - n_plain, 0))),
                  pl.BlockSpec((1, tn), lambda i, j: (0, jnp.maximum(j - n_plain, 0)))],
        out_specs=[pl.BlockSpec((tm, tn), lambda i, j: (i, j)),
                   pl.BlockSpec((tm, ndt), lambda i, j: (i, 0))],
        out_shape=[jax.ShapeDtypeStruct((t, n), BF16), jax.ShapeDtypeStruct((t, ndt), F32)],
        scratch_shapes=[pltpu.VMEM((tm + 2 * SUBLANES, d), BF16),
                        pltpu.VMEM((tm + 2 * SUBLANES, tn // CONV_COLUMN_PARTS), F32)],
        compiler_params=_params("arbitrary", "arbitrary"),
        name="m2_in_proj",
    )(x2, x2, x2, g, scale, shift, w_in, w_in, conv_w, conv_b)


N_PIECES = 3
GROUP_LANES = 128


def _piece_base(heads_per_group):
    w = N_PIECES * heads_per_group
    return {"cum_f": 0, "cum_b": w, "ecum_f": 2 * w, "ecum_b": 3 * w, "wend_f": 4 * w}


def _perm_matrices(n_heads, heads_per_group):
    rr = heads_per_group
    base = _piece_base(rr)
    nat = 2 * n_heads
    perm_a = np.zeros((3 * N_PIECES * nat, N_GROUPS * GROUP_LANES), np.float32)
    perm_w = np.zeros((N_PIECES * nat, N_GROUPS * GROUP_LANES), np.float32)
    for piece in range(N_PIECES):
        for d in range(2):
            for h in range(n_heads):
                g, r = divmod(h, rr)
                lane = piece * nat + d * n_heads + h
                col = g * GROUP_LANES + piece * rr + r
                perm_a[0 * N_PIECES * nat + lane, col + (base["cum_f"] if d == 0 else base["cum_b"])] = 1
                perm_a[1 * N_PIECES * nat + lane, col + (base["ecum_f"] if d == 0 else base["ecum_b"])] = 1
                if d == 0:
                    perm_a[2 * N_PIECES * nat + lane, col + base["wend_f"]] = 1
                else:
                    perm_w[lane, col] = 1
    return jnp.asarray(perm_a, BF16), jnp.asarray(perm_w, BF16)


def _select_matrices(heads_per_group, head_dim, q):
    rr = heads_per_group
    base = _piece_base(rr)
    gw = rr * head_dim
    sel_c = np.zeros((GROUP_LANES, 2 * rr * q), np.float32)
    sel_e = np.zeros((GROUP_LANES, 3 * gw), np.float32)
    sel_w = np.zeros((GROUP_LANES, gw), np.float32)
    for piece in range(N_PIECES):
        for r in range(rr):
            k = piece * rr + r
            sel_c[base["cum_f"] + k, r * q:(r + 1) * q] = 1
            sel_c[base["cum_b"] + k, (rr + r) * q:(rr + r + 1) * q] = 1
            sel_e[base["ecum_f"] + k, r * head_dim:(r + 1) * head_dim] = 1
            sel_e[base["ecum_b"] + k, gw + r * head_dim:gw + (r + 1) * head_dim] = 1
            sel_e[base["wend_f"] + k, 2 * gw + r * head_dim:2 * gw + (r + 1) * head_dim] = 1
            sel_w[k, r * head_dim:(r + 1) * head_dim] = 1
    return jnp.asarray(sel_c, BF16), jnp.asarray(sel_e, BF16), jnp.asarray(sel_w, BF16)


def _bf16_pieces(v):
    hi = v.astype(BF16)
    rem = v - hi.astype(F32)
    mid = rem.astype(BF16)
    lo = (rem - mid.astype(F32)).astype(BF16)
    return [hi, mid, lo]


def _dt_kernel(raw_ref, bias_ref, alog_ref, perm_a_ref, perm_w_ref,
               a_ref, w_ref, rowt_ref, diagt_ref, elast_ref, *, n_heads):
    q = CHUNK
    w = raw_ref.shape[1]
    r = lax.broadcasted_iota(jnp.int32, (q, q), 0)
    c = lax.broadcasted_iota(jnp.int32, (q, q), 1)
    lower = jnp.where(c <= r, 1.0, 0.0).astype(BF16)
    upper = jnp.where(c >= r, 1.0, 0.0).astype(BF16)
    fwd = lax.broadcasted_iota(jnp.int32, (q, w), 1) < n_heads
    neg_a2 = jnp.exp(alog_ref[...]) * LOG2_E
    lhs_a, lhs_w = [], []
    for k in range(raw_ref.shape[0] // q):
        rows = slice(k * q, (k + 1) * q)
        v = raw_ref[rows, :] + bias_ref[...]
        dt = jnp.maximum(v, 0.0) + jnp.log1p(jnp.exp(-jnp.abs(v)))
        a2 = -(dt * neg_a2)
        pieces = jnp.concatenate(_bf16_pieces(a2), axis=1)
        pre = _dot(lower, pieces)
        suf = _dot(upper, pieces)
        cum = jnp.where(fwd, pre[:, :w] + pre[:, w:2 * w] + pre[:, 2 * w:],
                        suf[:, :w] + suf[:, w:2 * w] + suf[:, 2 * w:])
        last = jnp.where(fwd[0:1, :], cum[q - 1:q, :], cum[0:1, :])
        wend = _bf16_pieces(jnp.exp2(last - cum) * dt)
        lhs_a.append(jnp.concatenate(_bf16_pieces(cum) + _bf16_pieces(jnp.exp2(cum)) + wend, axis=1))
        lhs_w.append(jnp.concatenate(wend, axis=1))
        rowt_ref[0, :, rows] = (cum - jnp.log2(dt)).T
        diagt_ref[0, :, rows] = jnp.log2(dt + pltpu.roll(dt, n_heads, 1)).T
        elast_ref[k] = jnp.exp2(last)
    a_ref[...] = _dot(jnp.concatenate(lhs_a, axis=0), perm_a_ref[...]).astype(BF16)
    w_ref[...] = _dot(jnp.concatenate(lhs_w, axis=0), perm_w_ref[...]).astype(BF16)


def _dt_prep(dt_raw, dt_bias, a_log, bsz, seq, heads_per_group):
    t, w = dt_raw.shape
    q = CHUNK
    nc = seq // q
    perm_a, perm_w = _perm_matrices(w // 2, heads_per_group)
    gl = N_GROUPS * GROUP_LANES
    per_step = DT_CHUNKS_PER_STEP
    assert nc % per_step == 0
    rows = per_step * q
    nblk = nc // per_step
    col_spec = pl.BlockSpec((rows, w), lambda i: (i, 0))
    vec_spec = pl.BlockSpec((1, w), lambda i: (0, 0))
    grp_spec = pl.BlockSpec((rows, gl), lambda i: (i, 0))
    return pl.pallas_call(
        functools.partial(_dt_kernel, n_heads=w // 2),
        grid=(t // rows,),
        in_specs=[col_spec, vec_spec, vec_spec,
                  pl.BlockSpec(perm_a.shape, lambda i: (0, 0)),
                  pl.BlockSpec(perm_w.shape, lambda i: (0, 0))],
        out_specs=[grp_spec, grp_spec,
                   pl.BlockSpec((1, w, rows), lambda i: (i // nblk, 0, i % nblk)),
                   pl.BlockSpec((1, w, rows), lambda i: (i // nblk, 0, i % nblk)),
                   pl.BlockSpec((per_step, 1, w), lambda i: (i, 0, 0))],
        out_shape=[jax.ShapeDtypeStruct((t, gl), BF16), jax.ShapeDtypeStruct((t, gl), BF16),
                   jax.ShapeDtypeStruct((bsz, w, seq), F32), jax.ShapeDtypeStruct((bsz, w, seq), F32),
                   jax.ShapeDtypeStruct((t // q, 1, w), F32)],
        compiler_params=_params("arbitrary"),
        name="m2_dt_prep",
    )(dt_raw, dt_bias, a_log, perm_a, perm_w)


def _ssd_backward_block(blk, x_ref, b_ref, w_ref, selw_ref, el_ref, stb, sb_scr, *, q, per_step):
    ks = list(reversed(range(per_step)))
    rows = [slice(k * q, (k + 1) * q) for k in range(per_step)]
    wexp = {k: _dot(w_ref[rows[k], :], selw_ref[...]) for k in ks}
    upd = {}
    for k in ks:
        xw = (x_ref[rows[k], :].astype(F32) * wexp[k]).astype(BF16)
        bt = b_ref[rows[k], :].astype(F32).T.astype(BF16)
        upd[k] = _dot(bt, xw)
    st = stb[...]
    for k in ks:
        sb_scr[blk * per_step + k] = st.astype(BF16)
        st = st * el_ref[0, k, 1:2, :] + upd[k]
    stb[...] = st


def _ssd_forward_block(blk, x_ref, b_ref, c_ref, z_ref, a_ref, selc_ref, sele_ref, rtf_ref, rtb_ref, dg_ref,
                       el_ref, dsk_ref, ng_ref, u_ref, stf, sb_scr, *, q, per_step, heads_per_group, head_dim):
    rr = heads_per_group
    gw = rr * head_dim
    ks = list(range(per_step))
    rows = [slice(k * q, (k + 1) * q) for k in ks]
    t_idx = lax.broadcasted_iota(jnp.int32, (q, q), 0)
    s_idx = lax.broadcasted_iota(jnp.int32, (q, q), 1)
    below = t_idx > s_idx
    above = t_idx < s_idx
    lane_head = lax.broadcasted_iota(jnp.int32, (q, 256), 1) // head_dim
    n_pairs = rr // 2
    heads_per_tile = 256 // head_dim

    scores, col_f, col_b, exp_f, exp_b, exp_w = {}, {}, {}, {}, {}, {}
    for k in ks:
        a = a_ref[rows[k], :]
        scores[k] = lax.dot_general(c_ref[rows[k], :], b_ref[rows[k], :], (((1,), (1,)), ((), ())),
                                    preferred_element_type=F32)
        for pr in range(n_pairs):
            col_f[k, pr] = _dot(a, selc_ref[:, 2 * pr * q:(2 * pr + 2) * q])
            col_b[k, pr] = _dot(a, selc_ref[:, (rr + 2 * pr) * q:(rr + 2 * pr + 2) * q])
        exp_f[k] = _dot(a, sele_ref[:, 0:gw])
        exp_b[k] = _dot(a, sele_ref[:, gw:2 * gw])
        exp_w[k] = _dot(a, sele_ref[:, 2 * gw:3 * gw])

    y, upd = {}, {}
    for k in ks:
        x = x_ref[rows[k], :]
        rtf, rtb, dg = rtf_ref[0, :, rows[k]], rtb_ref[0, :, rows[k]], dg_ref[0, :, rows[k]]
        y_parts = []
        for tile in range(gw // 256):
            xt = x[:, tile * 256:(tile + 1) * 256]
            acc = None
            for pair in range(heads_per_tile // 2):
                pr = tile * (heads_per_tile // 2) + pair
                ms, xs = [], []
                for j in range(2):
                    h = 2 * pr + j
                    cols = slice(j * q, (j + 1) * q)
                    seg = jnp.where(below, col_f[k, pr][:, cols] - rtf[h:h + 1, :],
                                    jnp.where(above, col_b[k, pr][:, cols] - rtb[h:h + 1, :], dg[h:h + 1, :]))
                    ms.append((scores[k] * jnp.exp2(seg)).astype(BF16))
                    xs.append(jnp.where(lane_head == 2 * pair + j, xt, jnp.zeros_like(xt)))
                part = _dot(jnp.concatenate(ms, axis=1), jnp.concatenate(xs, axis=0))
                acc = part if acc is None else acc + part
            y_parts.append(acc)
        xf32 = x.astype(F32)
        y[k] = (jnp.concatenate(y_parts, axis=1) + dsk_ref[...] * xf32
                + exp_b[k] * _dot(c_ref[rows[k], :], sb_scr[blk * per_step + k]))
        xw = (xf32 * exp_w[k]).astype(BF16)
        bt = b_ref[rows[k], :].astype(F32).T.astype(BF16)
        upd[k] = _dot(bt, xw)

    st = stf[...]
    sts = {}
    for k in ks:
        sts[k] = st.astype(BF16)
        st = st * el_ref[0, k, 0:1, :] + upd[k]
    stf[...] = st
    for k in ks:
        yk = y[k] + exp_f[k] * _dot(c_ref[rows[k], :], sts[k])
        u = yk * _silu(z_ref[rows[k], :].astype(F32))
        u_ref[rows[k], :] = _rms(u, ng_ref[...]).astype(BF16)


def _ssd_kernel(x_ref, b_ref, c_ref, z_ref, a_ref, w_ref, selc_ref, sele_ref, selw_ref,
                rtf_ref, rtb_ref, dg_ref, el_ref, dsk_ref, ng_ref,
                u_ref, stf, stb, sb_scr, *, heads_per_group, head_dim, per_step):
    q = x_ref.shape[0] // per_step
    phase = pl.program_id(2)
    step = pl.program_id(3)
    nblk = pl.num_programs(3)

    @pl.when(phase == 0)
    def _():
        @pl.when(step == 0)
        def _():
            stb[...] = jnp.zeros_like(stb)

        _ssd_backward_block(nblk - 1 - step, x_ref, b_ref, w_ref, selw_ref, el_ref, stb, sb_scr,
                            q=q, per_step=per_step)

    @pl.when(phase == 1)
    def _():
        @pl.when(step == 0)
        def _():
            stf[...] = jnp.zeros_like(stf)

        _ssd_forward_block(step, x_ref, b_ref, c_ref, z_ref, a_ref, selc_ref, sele_ref, rtf_ref, rtb_ref, dg_ref,
                           el_ref, dsk_ref, ng_ref, u_ref, stf, sb_scr, q=q, per_step=per_step,
                           heads_per_group=heads_per_group, head_dim=head_dim)


def _ssd(zxbc, a_cols, w_cols, row_t, diag_t, elast, dskip, norm_g, bsz, seq, d_inner, d_state, n_heads,
         to_cast):
    t = zxbc.shape[0]
    q = CHUNK
    nc = seq // q
    per_step = SSD_CHUNKS_PER_STEP
    assert nc % per_step == 0
    nblk = nc // per_step
    rows = per_step * q
    g = N_GROUPS
    rr = n_heads // g
    p = d_inner // n_heads
    gw = rr * p
    assert p == 64 and d_state == 128 and q == 128 and gw % 256 == 0 and rr % 2 == 0
    assert 5 * N_PIECES * rr <= GROUP_LANES and d_inner // g == gw
    xoff = d_inner // gw
    boff = 2 * d_inner // d_state
    sel_c, sel_e, sel_w = _select_matrices(rr, p, q)
    elast = elast.reshape(bsz * nblk, per_step, 2, d_inner)

    def both(b, gi, ph, s):
        return b * nblk + ph * s + (1 - ph) * (nblk - 1 - s)

    def fwd(b, gi, ph, s):
        return b * nblk + ph * s

    def bwd(b, gi, ph, s):
        return b * nblk + (1 - ph) * (nblk - 1 - s)

    const = lambda arr: pl.BlockSpec(arr.shape, lambda b, gi, ph, s: (0, 0))
    grid = (bsz, g, 2, nblk)
    cast_in, cast_out, cast_shape = _cast_args(to_cast, bsz * g * 2 * nblk,
                                               lambda b, gi, ph, s: ((b * g + gi) * 2 + ph) * nblk + s)
    outs = pl.pallas_call(
        _with_casts(functools.partial(_ssd_kernel, heads_per_group=rr, head_dim=p, per_step=per_step),
                    15, 1, len(to_cast)),
        grid=grid,
        in_specs=[pl.BlockSpec((rows, gw), lambda b, gi, ph, s: (both(b, gi, ph, s), xoff + gi)),
                  pl.BlockSpec((rows, d_state), lambda b, gi, ph, s: (both(b, gi, ph, s), boff + gi)),
                  pl.BlockSpec((rows, d_state), lambda b, gi, ph, s: (fwd(b, gi, ph, s), boff + g + gi)),
                  pl.BlockSpec((rows, gw), lambda b, gi, ph, s: (fwd(b, gi, ph, s), gi)),
                  pl.BlockSpec((rows, GROUP_LANES), lambda b, gi, ph, s: (fwd(b, gi, ph, s), gi)),
                  pl.BlockSpec((rows, GROUP_LANES), lambda b, gi, ph, s: (bwd(b, gi, ph, s), gi)),
                  const(sel_c), const(sel_e), const(sel_w),
                  pl.BlockSpec((1, rr, rows), lambda b, gi, ph, s: (b, gi, ph * s)),
                  pl.BlockSpec((1, rr, rows), lambda b, gi, ph, s: (b, g + gi, ph * s)),
                  pl.BlockSpec((1, rr, rows), lambda b, gi, ph, s: (b, gi, ph * s)),
                  pl.BlockSpec((1, per_step, 2, gw), lambda b, gi, ph, s: (both(b, gi, ph, s), 0, 0, gi)),
                  pl.BlockSpec((1, gw), lambda b, gi, ph, s: (0, gi)),
                  pl.BlockSpec((1, gw), lambda b, gi, ph, s: (0, gi))] + cast_in,
        out_specs=[pl.BlockSpec((rows, gw), lambda b, gi, ph, s: (fwd(b, gi, ph, s), gi))] + cast_out,
        out_shape=[jax.ShapeDtypeStruct((t, d_inner), BF16)] + cast_shape,
        scratch_shapes=[pltpu.VMEM((d_state, gw), F32), pltpu.VMEM((d_state, gw), F32),
                        pltpu.VMEM((nc, d_state, gw), BF16)],
        compiler_params=_params("arbitrary", "arbitrary", "arbitrary", "arbitrary"),
        name="m2_ssd",
    )(zxbc, zxbc, zxbc, zxbc, a_cols, w_cols, sel_c, sel_e, sel_w, row_t, row_t, diag_t, elast, dskip, norm_g,
      *[spec[0] for spec in to_cast])
    return outs[0], outs[1:]


def _m2_out_kernel(u_ref, w_ref, x_ref, gate_ref, o_ref):
    o_ref[...] = x_ref[...] + gate_ref[0] * _dot(u_ref[...], w_ref[...])


def _m2_out(u, w_out, x2, gate, seq, tm, tn):
    t, d = x2.shape
    di = u.shape[1]
    tps = seq // tm
    return pl.pallas_call(
        _m2_out_kernel,
        grid=(t // tm, d // tn),
        in_specs=[pl.BlockSpec((tm, di), lambda i, j: (i, 0)),
                  pl.BlockSpec((di, tn), lambda i, j: (0, j)),
                  pl.BlockSpec((tm, tn), lambda i, j: (i, j)),
                  pl.BlockSpec((1, 1, tn), lambda i, j: (i // tps, 0, j))],
        out_specs=pl.BlockSpec((tm, tn), lambda i, j: (i, j)),
        out_shape=jax.ShapeDtypeStruct((t, d), F32),
        compiler_params=_params("arbitrary", "arbitrary"),
        name="m2_out_proj",
    )(u, w_out, x2, gate)


def _row_tile(seq, want):
    tm = min(want, seq)
    assert seq % tm == 0 and tm % HALO_ROWS == 0
    return tm


def _tiles(seq):
    return dict(tm=_row_tile(seq, 1024), tm_conv=_row_tile(seq, 512), tn=512, tn_wide=1024, tf=1024)


def kernel(x, c, ada_w, ada_b, norm_g, final_g, sc_in_w, sc_conv_w, sc_out_w, m2_in_w, m2_conv_w, m2_conv_b,
           m2_dt_bias, m2_a_log, m2_d, m2_norm_g, m2_out_w, mlp_w1, mlp_w2):
    bsz, seq, d = x.shape
    t = bsz * seq
    d_inner = m2_norm_g.shape[1]
    n_heads = m2_d.shape[1]
    conv_dim = m2_conv_w.shape[1]
    d_state = (conv_dim - d_inner) // (2 * N_GROUPS)
    head_dim = d_inner // n_heads
    assert seq % CHUNK == 0
    ts = _tiles(seq)
    tm, tn, tf = ts["tm"], ts["tn"], ts["tf"]

    mods = _modulation(c, ada_w, ada_b)
    x2 = x.reshape(t, d)
    vec = lambda v: v.reshape(1, -1)

    shift, scale, gate = mods[0]
    bg, cv, (sc_out_bf, w1_bf, w2_bf) = _sc_in(x2, vec(norm_g[0, 0]), scale, shift, sc_in_w[0].astype(BF16),
                                               seq, tm, tn, [(sc_out_w, 0, None), (mlp_w1, 0, tf), (mlp_w2, 0, None)])
    x2 = _sc_out(cv, bg, sc_conv_w[0].T, sc_out_bf, x2, gate, seq, ts["tm_conv"])
    shift, scale, gate = mods[1]
    x2, (m2_in_bf,) = _mlp(x2, vec(norm_g[0, 1]), scale, shift, gate, w1_bf, w2_bf,
                           vec(final_g), seq, tm, final_norm=False, to_cast=[(m2_in_w, 0, None)])

    shift, scale, gate = mods[2]
    zxbc, dt_raw = _m2_in(x2, vec(norm_g[1, 0]), scale, shift, m2_in_bf, m2_conv_w[0].T, vec(m2_conv_b[0]),
                          d_inner, seq, tm, ts["tn_wide"])
    rr = n_heads // N_GROUPS
    a_cols, w_cols, row_t, diag_t, elast = _dt_prep(dt_raw, vec(m2_dt_bias[0]), vec(m2_a_log[0]), bsz, seq, rr)
    elast = jnp.repeat(elast.reshape(-1, 2, n_heads), head_dim, axis=2)
    dskip = jnp.repeat(m2_d[0], head_dim).reshape(1, d_inner)
    u, (m2_out_bf, w1_bf, w2_bf) = _ssd(zxbc, a_cols, w_cols, row_t, diag_t, elast, dskip, vec(m2_norm_g[0]),
                                        bsz, seq, d_inner, d_state, n_heads,
                                        [(m2_out_w, 0, None), (mlp_w1, 1, tf), (mlp_w2, 1, None)])
    x2 = _m2_out(u, m2_out_bf, x2, gate, seq, tm, ts["tn_wide"])
    shift, scale, gate = mods[3]
    x2, _ = _mlp(x2, vec(norm_g[1, 1]), scale, shift, gate, w1_bf, w2_bf,
                 vec(final_g), seq, tm, final_norm=True)
    return x2.reshape(bsz, seq, d)
```
